```python
import jax, jax.numpy as jnp
from jax import lax
import numpy as np

D_MODEL = 2048
BATCH = 4
SEQ = 4096
DEPTH = 2
DEC_BATCH = 128
DEC_SEQ = 8
PAST_LEN = 16384
PAGE_SIZE = 128

D_MIX = D_MODEL
GROUP_W = D_MIX // 4
H_A = 4
NOPE_A = 128
ROPE_A = 64
V_A = GROUP_W // H_A
Q_LORA = 384
KV_LORA = 256
MLA_SCALE = (NOPE_A + ROPE_A) ** -0.5
H_B = 8
HD_B = GROUP_W // H_B
ROT_B = HD_B // 4
L_CMP = 32
D_CMP = 16
L_SEL = 64
N_SEL = 16
WINDOW = 512
NSA_SCALE = HD_B ** -0.5
FORCE = 1e9
POOL_WINDOWS = (2, 4, 8, 16)
POOL_GW = GROUP_W // 4
POOL_BUF = 15
CHUNK = 128
G_D = 4
GD_W = GROUP_W // G_D
MEM_LEN = 256
H_M = 4
HD_M = 128
MEM_SCALE = HD_M ** -0.5
D_FF = 5632
CONV_W = 3
ROPE_THETA = 500000.0
EPS = 1e-6
Q_BLOCK = 128
IN_SIZES = (Q_LORA, KV_LORA, ROPE_A, H_B * HD_B, 6 * HD_B, 3 * H_B, GROUP_W, 2 * GROUP_W)
P_IN = Q_LORA + KV_LORA + ROPE_A + H_B * HD_B + 6 * HD_B + 3 * H_B + GROUP_W + 2 * GROUP_W

kernel_name = 'hybrid_mla_nsa_pool_gmlp_decode_step'


def rmsnorm(x, g):
    xf = x.astype(jnp.float32)
    y = xf * lax.rsqrt(jnp.mean(xf * xf, axis=-1, keepdims=True) + EPS)
    return (y * g.astype(jnp.float32)).astype(x.dtype)


def group_rmsnorm(x, g, n_groups):
    xs = x.reshape(x.shape[:-1] + (n_groups, -1))
    return rmsnorm(xs, g.reshape(n_groups, -1)).reshape(x.shape)


def rope(x, pos, rot):
    half = rot // 2
    inv = ROPE_THETA ** (-2.0 * jnp.arange(half, dtype=jnp.float32) / rot)
    ang = pos.astype(jnp.float32)[:, None] * inv[None, :]
    shape = (pos.shape[0],) + (1,) * (x.ndim - 3) + (half,)
    cos, sin = jnp.cos(ang).reshape(shape), jnp.sin(ang).reshape(shape)
    xf = x.astype(jnp.float32)
    x1, x2 = xf[..., :half], xf[..., half:rot]
    return jnp.concatenate([x1 * cos - x2 * sin, x2 * cos + x1 * sin, xf[..., rot:]], axis=-1).astype(x.dtype)


def masked_softmax(s, mask):
    s = jnp.where(mask, s.astype(jnp.float32), -jnp.inf)
    m = jnp.max(s, axis=-1, keepdims=True)
    e = jnp.exp(s - jnp.where(jnp.isfinite(m), m, 0.0))
    return e / jnp.maximum(jnp.sum(e, axis=-1, keepdims=True), 1e-30)


def project(h, pos, w_in, q_norm, kv_norm, w_uq, w_uk, gate_bias):
    B, T = h.shape[0], h.shape[1]
    z = h @ w_in
    idx, acc = [], 0
    for n in IN_SIZES[:-1]:
        acc += n
        idx.append(acc)
    c_q, c_kv, k_pe, q_b, kv_b, g_b, x_c, x_d = jnp.split(z, idx, axis=-1)
    qa = jnp.einsum('btr,rhd->bthd', rmsnorm(c_q, q_norm), w_uq)
    q_abs = jnp.einsum('bthd,rhd->bthr', qa[..., :NOPE_A], w_uk)
    q_pe = rope(qa[..., NOPE_A:], pos, ROPE_A)
    ckv = rmsnorm(c_kv, kv_norm)
    kpe = rope(k_pe, pos, ROPE_A)
    qn = rope(q_b.reshape(B, T, H_B, HD_B), pos, ROT_B)
    kv = kv_b.reshape(B, T, 3, 2, HD_B)
    kv = jnp.stack([rope(kv[:, :, :, 0], pos, ROT_B), kv[:, :, :, 1]], axis=3).reshape(B, T, 6, HD_B)
    gates = jax.nn.sigmoid(g_b + gate_bias).reshape(B, T, H_B, 3)
    return q_abs, q_pe, ckv, kpe, qn, kv, gates, x_c, x_d


def mla_attend(q_abs, q_pe, ckv, kpe, mask):
    s = (jnp.einsum('...thr,...kr->...htk', q_abs, ckv)
         + jnp.einsum('...thp,...kp->...htk', q_pe, kpe)) * MLA_SCALE
    p = masked_softmax(s, mask)
    return jnp.einsum('...htk,...kr->...thr', p.astype(ckv.dtype), ckv)


def mla_prompt(q_abs, q_pe, ckv, kpe):
    B, S = q_abs.shape[0], q_abs.shape[1]
    nb = S // Q_BLOCK
    qb = q_abs.reshape(B, nb, Q_BLOCK, H_A, KV_LORA).swapaxes(0, 1)
    pb = q_pe.reshape(B, nb, Q_BLOCK, H_A, ROPE_A).swapaxes(0, 1)
    k_pos = jnp.arange(S)

    def block(args):
        qa, qp, i = args
        q_pos = i * Q_BLOCK + jnp.arange(Q_BLOCK)
        return mla_attend(qa, qp, ckv, kpe, k_pos[None, :] <= q_pos[:, None])

    o = lax.map(block, (qb, pb, jnp.arange(nb)))
    return o.swapaxes(0, 1).reshape(B, S, H_A, KV_LORA)


def mla_sample(q_abs, q_pe, ckv_new, kpe_new, cache, layer, page_table):
    T = q_abs.shape[1]
    q_pos = PAST_LEN + jnp.arange(T)
    k_pos = jnp.arange(PAST_LEN + T)
    mask = k_pos[None, :] <= q_pos[:, None]

    def one(args):
        qa, qp, cn, kn, pt = args
        rows = cache[layer, pt].reshape(-1, KV_LORA + ROPE_A)
        ckv = jnp.concatenate([rows[:, :KV_LORA].astype(cn.dtype), cn], axis=0)
        kpe = jnp.concatenate([rows[:, KV_LORA:].astype(kn.dtype), kn], axis=0)
        return mla_attend(qa, qp, ckv, kpe, mask)

    return lax.map(one, (q_abs, q_pe, ckv_new, kpe_new, page_table))


def mla_out(o_lat, w_uv):
    B, T = o_lat.shape[0], o_lat.shape[1]
    return jnp.einsum('bthr,rhd->bthd', o_lat, w_uv).reshape(B, T, GROUP_W)


def nsa_compress(k, pe, w1, b1, w2):
    n_cmp = (k.shape[0] - L_CMP) // D_CMP + 1
    idx = (jnp.arange(n_cmp) * D_CMP)[:, None] + jnp.arange(L_CMP)[None, :]
    hid = jax.nn.gelu((k[idx] + pe) @ w1 + b1)
    return hid.reshape(n_cmp, L_CMP * HD_B) @ w2


def compress_kv(k, v, pe, w1, b1, w2):
    return (nsa_compress(k, pe[0], w1[0], b1[0], w2[0]), nsa_compress(v, pe[1], w1[1], b1[1], w2[1]))


def nsa_cmp_sel(q, q_pos, kc, vc, k_sel, v_sel):
    T = q.shape[0]
    n_cmp, n = kc.shape[0], k_sel.shape[0]
    n_blk = -(-n // L_SEL)
    c_end = jnp.arange(n_cmp) * D_CMP + (L_CMP - 1)
    p_c = masked_softmax(jnp.einsum('thd,nd->htn', q, kc) * NSA_SCALE, c_end[None, :] <= q_pos[:, None])
    o_cmp = jnp.einsum('htn,nd->thd', p_c.astype(vc.dtype), vc)
    ci = jnp.arange(n_cmp)[:, None] * D_CMP
    bj = jnp.arange(n_blk)[None, :] * L_SEL
    cover = ((ci < bj + L_SEL) & (ci + L_CMP > bj)).astype(jnp.float32)
    imp = jnp.sum(p_c, axis=0) @ cover
    cur = (q_pos // L_SEL)[:, None]
    j = jnp.arange(n_blk)[None, :]
    causal = j <= cur
    forced = (j == 0) | (j == cur) | (j == cur - 1)
    score = jnp.where(forced, FORCE, jnp.where(causal, imp, -FORCE))
    k_top = min(N_SEL, n_blk)
    _, sel = lax.top_k(score, k_top)
    sel_ok = jnp.take_along_axis(causal, sel, axis=1)
    pos = (sel[:, :, None] * L_SEL + jnp.arange(L_SEL)[None, None, :]).reshape(T, k_top * L_SEL)
    ok = jnp.repeat(sel_ok, L_SEL, axis=1) & (pos <= q_pos[:, None])
    pos_c = jnp.minimum(pos, n - 1)
    kg, vg = k_sel[pos_c], v_sel[pos_c]
    p_s = masked_softmax(jnp.einsum('thd,tkd->htk', q, kg) * NSA_SCALE, ok[None])
    o_sel = jnp.einsum('htk,tkd->thd', p_s.astype(vg.dtype), vg)
    return o_cmp, o_sel


def window_attend(q, q_pos, k, v, k_pos):
    ok = ((k_pos[None, :] <= q_pos[:, None]) & (k_pos[None, :] > q_pos[:, None] - WINDOW)
          & (k_pos[None, :] >= 0))
    p = masked_softmax(jnp.einsum('bthd,bkd->bhtk', q, k) * NSA_SCALE, ok)
    return jnp.einsum('bhtk,bkd->bthd', p.astype(v.dtype), v)


def nsa_prompt(q, kv, pe, w1, b1, w2):
    B, S = q.shape[0], q.shape[1]
    kc, vc = jax.vmap(compress_kv, in_axes=(0, 0, None, None, None, None))(kv[:, :, 0], kv[:, :, 1], pe, w1, b1, w2)
    k_sel, v_sel = kv[:, :, 2], kv[:, :, 3]
    pad = ((0, 0), (WINDOW, 0), (0, 0))
    k_win, v_win = jnp.pad(kv[:, :, 4], pad), jnp.pad(kv[:, :, 5], pad)
    nb = S // Q_BLOCK
    qb = q.reshape(B, nb, Q_BLOCK, H_B, HD_B).swapaxes(0, 1)
    sparse = jax.vmap(nsa_cmp_sel, in_axes=(0, None, 0, 0, 0, 0))

    def block(args):
        qq, i = args
        start = i * Q_BLOCK
        q_pos = start + jnp.arange(Q_BLOCK)
        o_c, o_s = sparse(qq, q_pos, kc, vc, k_sel, v_sel)
        k_pos = start - WINDOW + jnp.arange(WINDOW + Q_BLOCK)
        kb = lax.dynamic_slice_in_dim(k_win, start, WINDOW + Q_BLOCK, axis=1)
        vb = lax.dynamic_slice_in_dim(v_win, start, WINDOW + Q_BLOCK, axis=1)
        return o_c, o_s, window_attend(qq, q_pos, kb, vb, k_pos)

    o_c, o_s, o_w = lax.map(block, (qb, jnp.arange(nb)))
    unblock = lambda o: o.swapaxes(0, 1).reshape(B, S, H_B, HD_B)
    return unblock(o_c), unblock(o_s), unblock(o_w)


def nsa_sample(q, kv, cache, layer, page_table, win_buf, pe, w1, b1, w2):
    T = q.shape[1]
    q_pos = PAST_LEN + jnp.arange(T)

    def one(args):
        qq, kn, pt = args
        rows = cache[layer, pt].reshape(-1, 4, HD_B).astype(kn.dtype)
        full = jnp.concatenate([rows, kn[:, :4]], axis=0)
        kc, vc = compress_kv(full[:, 0], full[:, 1], pe, w1, b1, w2)
        return nsa_cmp_sel(qq, q_pos, kc, vc, full[:, 2], full[:, 3])

    o_c, o_s = lax.map(one, (q, kv, page_table))
    lb = win_buf.shape[1]
    wkv = jnp.concatenate([win_buf.astype(kv.dtype), kv[:, :, 4:6]], axis=1)
    k_pos = PAST_LEN - lb + jnp.arange(lb + T)
    o_w = window_attend(q, q_pos, wkv[:, :, 0], wkv[:, :, 1], k_pos)
    return o_c, o_s, o_w, wkv[:, -lb:]


def nsa_merge(gates, o_c, o_s, o_w):
    B, T = o_c.shape[0], o_c.shape[1]
    o = gates[..., 0:1] * o_c + gates[..., 1:2] * o_s + gates[..., 2:3] * o_w
    return o.reshape(B, T, GROUP_W)


def pool_mix(xs, w, scale):
    B, L = xs.shape[0], xs.shape[1]
    xf = xs.astype(jnp.float32)
    cs0 = jnp.pad(jnp.cumsum(xf, axis=1), ((0, 0), (1, 0), (0, 0)))
    r = jnp.arange(L)
    parts = []
    for g, win in enumerate(POOL_WINDOWS):
        c0, c1 = g * POOL_GW, (g + 1) * POOL_GW
        start = jnp.maximum(r + 1 - win, 0)
        cnt = (r + 1 - start).astype(jnp.float32)[None, :, None]
        mean = (cs0[:, 1:, c0:c1] - cs0[:, start, c0:c1]) / cnt
        parts.append(mean - xf[:, :, c0:c1])
    pooled = jnp.stack(parts, axis=2).astype(xs.dtype)
    y = jnp.einsum('blgc,gcd->blgd', pooled, w).reshape(B, L, GROUP_W)
    return y * scale


def gmlp_mix(zd, v_norm, ws, bs):
    B, L = zd.shape[0], zd.shape[1]
    u, v = jnp.split(jax.nn.gelu(zd), 2, axis=-1)
    v = rmsnorm(v, v_norm)
    n = min(L, CHUNK)
    wm = ws[:, :n, :n] * jnp.tril(jnp.ones((n, n), ws.dtype))
    vc = v.reshape(B, L // n, n, G_D, GD_W)
    s = jnp.einsum('gts,bcsgd->bctgd', wm, vc) + bs[:, :n].T[None, None, :, :, None]
    return u * s.reshape(B, L, GROUP_W), v


def mix_merge(oa, ob, oc, od, g, w_out):
    o = jnp.concatenate([oa, ob, oc, od], axis=-1)
    return group_rmsnorm(o, g, 4) @ w_out


def mem_attend(h, mkv, wq, wo):
    B, T = h.shape[0], h.shape[1]
    q = (h @ wq).reshape(B, T, H_M, HD_M)
    k = mkv[:, :, 0].reshape(B, -1, H_M, HD_M)
    v = mkv[:, :, 1].reshape(B, -1, H_M, HD_M)
    p = jax.nn.softmax(jnp.einsum('bthd,bmhd->bhtm', q, k).astype(jnp.float32) * MEM_SCALE, axis=-1)
    o = jnp.einsum('bhtm,bmhd->bthd', p.astype(v.dtype), v).reshape(B, T, H_M * HD_M)
    return o @ wo


def conv_ffn(h, buf, w_up, conv_w, conv_b, w_down):
    a = h @ w_up
    L = a.shape[1]
    ap = jnp.concatenate([buf.astype(a.dtype), a], axis=1)
    c = conv_b
    for k in range(CONV_W):
        c = c + ap[:, k:k + L] * conv_w[k]
    g, u = jnp.split(c, 2, axis=-1)
    return (jax.nn.silu(g) * u) @ w_down, ap[:, L:]


def setup_inputs(seed: int = 0) -> dict:
    key = jax.random.key(seed)
    ks = iter(jax.random.split(key, 64))

    def nrm(shape, scale=1.0):
        return jax.random.normal(next(ks), shape, jnp.float32) * scale

    def gain(shape):
        return 1.0 + 0.02 * nrm(shape)

    n_pages = PAST_LEN // PAGE_SIZE
    n_pool = (DEC_BATCH * n_pages * 5) // 4
    page_table = jax.random.permutation(next(ks), n_pool)[:DEC_BATCH * n_pages].reshape(DEC_BATCH, n_pages).astype(jnp.int32)
    win_len = min(WINDOW, PAST_LEN)
    return {
        'x_prompt': nrm((BATCH, SEQ, D_MODEL)),
        'x_sample': nrm((DEC_BATCH, DEC_SEQ, D_MODEL)),
        'cache_mla': nrm((DEPTH, n_pool, PAGE_SIZE, KV_LORA + ROPE_A)),
        'cache_nsa': nrm((DEPTH, n_pool, PAGE_SIZE, 4, HD_B)),
        'state_nsa_win': nrm((DEPTH, DEC_BATCH, win_len, 2, HD_B)),
        'state_pool': nrm((DEPTH, DEC_BATCH, POOL_BUF, GROUP_W)),
        'state_conv': nrm((DEPTH, DEC_BATCH, CONV_W - 1, 2 * D_FF)),
        'cache_mem_kv': nrm((DEPTH, DEC_BATCH, MEM_LEN, 2, H_M * HD_M)),
        'page_table': page_table,
        'mem_prompt': nrm((BATCH, MEM_LEN, D_MODEL)),
        'mix_norm': gain((DEPTH, D_MODEL)),
        'w_in': nrm((DEPTH, D_MODEL, P_IN), D_MODEL ** -0.5),
        'mla_q_norm': gain((DEPTH, Q_LORA)),
        'mla_kv_norm': gain((DEPTH, KV_LORA)),
        'mla_w_uq': nrm((DEPTH, Q_LORA, H_A, NOPE_A + ROPE_A), Q_LORA ** -0.5),
        'mla_w_ukv': nrm((DEPTH, KV_LORA, H_A, NOPE_A + V_A), KV_LORA ** -0.5),
        'nsa_gate_bias': nrm((DEPTH, 3 * H_B), 0.1),
        'nsa_cmp_pe': nrm((DEPTH, 2, L_CMP, HD_B), 0.1),
        'nsa_cmp_w1': nrm((DEPTH, 2, HD_B, HD_B), HD_B ** -0.5),
        'nsa_cmp_b1': nrm((DEPTH, 2, HD_B), 0.02),
        'nsa_cmp_w2': nrm((DEPTH, 2, L_CMP * HD_B, HD_B), (L_CMP * HD_B) ** -0.5),
        'pool_w': nrm((DEPTH, 4, POOL_GW, POOL_GW), POOL_GW ** -0.5),
        'pool_scale': gain((DEPTH, GROUP_W)),
        'gmlp_v_norm': gain((DEPTH, GROUP_W)),
        'gmlp_ws': nrm((DEPTH, G_D, CHUNK, CHUNK), CHUNK ** -0.5),
        'gmlp_bs': 1.0 + 0.1 * nrm((DEPTH, G_D, CHUNK)),
        'mix_out_norm': gain((DEPTH, D_MIX)),
        'w_out': nrm((DEPTH, D_MIX, D_MODEL), D_MIX ** -0.5),
        'mem_norm': gain((DEPTH, D_MODEL)),
        'mem_kv_norm': gain((DEPTH, D_MODEL)),
        'mem_wq': nrm((DEPTH, D_MODEL, H_M * HD_M), D_MODEL ** -0.5),
        'mem_wkv': nrm((DEPTH, D_MODEL, 2 * H_M * HD_M), D_MODEL ** -0.5),
        'mem_wo': nrm((DEPTH, H_M * HD_M, D_MODEL), (H_M * HD_M) ** -0.5),
        'ffn_norm': gain((DEPTH, D_MODEL)),
        'ffn_w_up': nrm((DEPTH, D_MODEL, 2 * D_FF), D_MODEL ** -0.5),
        'ffn_conv_w': nrm((DEPTH, CONV_W, 2 * D_FF), CONV_W ** -0.5),
        'ffn_conv_b': nrm((DEPTH, 2 * D_FF), 0.02),
        'ffn_w_down': nrm((DEPTH, D_FF, D_MODEL), D_FF ** -0.5),
        'final_norm': gain((D_MODEL,)),
    }


def reference(x_prompt, x_sample, cache_mla, cache_nsa, state_nsa_win, state_pool, state_conv, cache_mem_kv,
              page_table, mem_prompt, mix_norm, w_in, mla_q_norm, mla_kv_norm, mla_w_uq, mla_w_ukv,
              nsa_gate_bias, nsa_cmp_pe, nsa_cmp_w1, nsa_cmp_b1, nsa_cmp_w2, pool_w, pool_scale,
              gmlp_v_norm, gmlp_ws, gmlp_bs, mix_out_norm, w_out, mem_norm, mem_kv_norm, mem_wq, mem_wkv,
              mem_wo, ffn_norm, ffn_w_up, ffn_conv_w, ffn_conv_b, ffn_w_down, final_norm):
    B, S = x_prompt.shape[0], x_prompt.shape[1]
    DB, T = x_sample.shape[0], x_sample.shape[1]
    pos_p = jnp.arange(S, dtype=jnp.int32)
    pos_s = PAST_LEN + jnp.arange(T, dtype=jnp.int32)
    lb_p = min(WINDOW, S)
    xp, xs = x_prompt, x_sample
    mla_p, mla_s, nsa_p, nsa_s, win_p, win_s = [], [], [], [], [], []
    pool_p, pool_s, gv_s, conv_p, conv_s, memkv_p = [], [], [], [], [], []
    for l in range(DEPTH):
        w_uk = mla_w_ukv[l][..., :NOPE_A]
        w_uv = mla_w_ukv[l][..., NOPE_A:]
        cmp = (nsa_cmp_pe[l], nsa_cmp_w1[l], nsa_cmp_b1[l], nsa_cmp_w2[l])

        hp = rmsnorm(xp, mix_norm[l])
        qa, qpe, ckv, kpe, qb, kv, gt, xc, xd = project(hp, pos_p, w_in[l], mla_q_norm[l], mla_kv_norm[l], mla_w_uq[l], w_uk, nsa_gate_bias[l])
        oa = mla_out(mla_prompt(qa, qpe, ckv, kpe), w_uv)
        o_c, o_s, o_w = nsa_prompt(qb, kv, *cmp)
        ob = nsa_merge(gt, o_c, o_s, o_w)
        oc = pool_mix(xc, pool_w[l], pool_scale[l])
        od, _ = gmlp_mix(xd, gmlp_v_norm[l], gmlp_ws[l], gmlp_bs[l])
        xp = xp + mix_merge(oa, ob, oc, od, mix_out_norm[l], w_out[l])
        mla_p.append(jnp.concatenate([ckv, kpe], axis=-1))
        nsa_p.append(kv[:, :, :4])
        win_p.append(kv[:, S - lb_p:, 4:6])
        pool_p.append(xc[:, S - POOL_BUF:])

        hs = rmsnorm(xs, mix_norm[l])
        qa, qpe, ckv, kpe, qb, kv, gt, xc, xd = project(hs, pos_s, w_in[l], mla_q_norm[l], mla_kv_norm[l], mla_w_uq[l], w_uk, nsa_gate_bias[l])
        oa = mla_out(mla_sample(qa, qpe, ckv, kpe, cache_mla, l, page_table), w_uv)
        o_c, o_s, o_w, new_win = nsa_sample(qb, kv, cache_nsa, l, page_table, state_nsa_win[l], *cmp)
        ob = nsa_merge(gt, o_c, o_s, o_w)
        xcat = jnp.concatenate([state_pool[l].astype(xc.dtype), xc], axis=1)
        oc = pool_mix(xcat, pool_w[l], pool_scale[l])[:, POOL_BUF:]
        od, v_rows = gmlp_mix(xd, gmlp_v_norm[l], gmlp_ws[l], gmlp_bs[l])
        xs = xs + mix_merge(oa, ob, oc, od, mix_out_norm[l], w_out[l])
        mla_s.append(jnp.concatenate([ckv, kpe], axis=-1))
        nsa_s.append(kv[:, :, :4])
        win_s.append(new_win)
        pool_s.append(xcat[:, xcat.shape[1] - POOL_BUF:])
        gv_s.append(v_rows)

        mkv = (rmsnorm(mem_prompt, mem_kv_norm[l]) @ mem_wkv[l]).reshape(B, mem_prompt.shape[1], 2, H_M * HD_M)
        xp = xp + mem_attend(rmsnorm(xp, mem_norm[l]), mkv, mem_wq[l], mem_wo[l])
        xs = xs + mem_attend(rmsnorm(xs, mem_norm[l]), cache_mem_kv[l].astype(xs.dtype), mem_wq[l], mem_wo[l])
        memkv_p.append(mkv)

        yp, cbp = conv_ffn(rmsnorm(xp, ffn_norm[l]), jnp.zeros((B, CONV_W - 1, 2 * D_FF), xp.dtype), ffn_w_up[l], ffn_conv_w[l], ffn_conv_b[l], ffn_w_down[l])
        ys, cbs = conv_ffn(rmsnorm(xs, ffn_norm[l]), state_conv[l], ffn_w_up[l], ffn_conv_w[l], ffn_conv_b[l], ffn_w_down[l])
        xp = xp + yp
        xs = xs + ys
        conv_p.append(cbp)
        conv_s.append(cbs)

    y_prompt = rmsnorm(xp, final_norm)
    y_sample = rmsnorm(xs, final_norm)
    return (y_prompt, y_sample, jnp.stack(mla_p), jnp.stack(mla_s), jnp.stack(nsa_p), jnp.stack(nsa_s),
            jnp.stack(win_p), jnp.stack(win_s), jnp.stack(pool_p), jnp.stack(pool_s), jnp.stack(gv_s),
            jnp.stack(conv_p), jnp.stack(conv_s), jnp.stack(memkv_p))
```

```python
import functools

import jax
import jax.numpy as jnp
import numpy as np
from jax import lax
from jax.experimental import pallas as pl
from jax.experimental.pallas import tpu as pltpu

D_MODEL = 2048
DEPTH = 2
PAST_LEN = 16384
PAGE_SIZE = 128
GROUP_W = 512
H_A = 4
NOPE_A = 128
ROPE_A = 64
V_A = 128
Q_LORA = 384
KV_LORA = 256
MLA_SCALE = (NOPE_A + ROPE_A) ** -0.5
H_B = 8
HD_B = 64
ROT_B = 16
L_CMP = 32
D_CMP = 16
L_SEL = 64
N_SEL = 16
WINDOW = 512
NSA_SCALE = HD_B ** -0.5
FORCE = 1e9
POOL_WINDOWS = (2, 4, 8, 16)
POOL_GW = 128
POOL_BUF = 15
CHUNK = 128
G_D = 4
GD_W = 128
MEM_LEN = 256
H_M = 4
HD_M = 128
MEM_SCALE = HD_M ** -0.5
D_FF = 5632
CONV_W = 3
ROPE_THETA = 500000.0
EPS = 1e-6
Q_BLOCK = 128
IN_SIZES = (Q_LORA, KV_LORA, ROPE_A, H_B * HD_B, 6 * HD_B, 3 * H_B, GROUP_W, 2 * GROUP_W)
P_IN = sum(IN_SIZES)
P_IN_PAD = 3200

V7X_VMEM_BYTES = 64 * 1024 * 1024
LANE = 128
VMEM_LIMIT = 56 * 1024 * 1024


def _dense_kernel(*refs, n_groups, has_norm, has_res):
    it = iter(refs)
    x_ref = next(it)
    g_ref = next(it) if has_norm else None
    w_ref = next(it)
    r_ref = next(it) if has_res else None
    o_ref = next(it)
    xn_ref = next(it)

    @pl.when(pl.program_id(1) == 0)
    def _():
        x = x_ref[...]
        if has_norm:
            xf = x.astype(jnp.float32)
            gw = xf.shape[-1] // n_groups
            parts = []
            for gi in range(n_groups):
                xg = xf[:, gi * gw:(gi + 1) * gw]
                ms = jnp.mean(xg * xg, axis=-1, keepdims=True)
                parts.append(xg * lax.rsqrt(ms + EPS) * g_ref[:, gi * gw:(gi + 1) * gw])
            xf = parts[0] if n_groups == 1 else jnp.concatenate(parts, axis=-1)
            xn_ref[...] = xf.astype(jnp.bfloat16)
        else:
            xn_ref[...] = x.astype(jnp.bfloat16)

    acc = jnp.dot(xn_ref[...], w_ref[...], preferred_element_type=jnp.float32)
    if has_res:
        acc = acc + r_ref[...]
    o_ref[...] = acc


def _pick_tile(n, cap):
    best = None
    for t in range(LANE, min(n, cap) + 1, LANE):
        if n % t == 0:
            best = t
    return best if best is not None else n


def dense(x, w_bf16, norm_g=None, n_groups=1, residual=None, tm=512, tn_cap=1024):
    m, k = x.shape
    n = w_bf16.shape[1]
    tm = min(tm, m)
    assert m % tm == 0
    tn = _pick_tile(n, tn_cap)
    has_norm = norm_g is not None
    has_res = residual is not None
    in_specs = [pl.BlockSpec((tm, k), lambda i, j: (i, 0))]
    args = [x]
    if has_norm:
        in_specs.append(pl.BlockSpec((1, k), lambda i, j: (0, 0)))
        args.append(norm_g.reshape(1, k).astype(jnp.float32))
    in_specs.append(pl.BlockSpec((k, tn), lambda i, j: (0, j)))
    args.append(w_bf16)
    if has_res:
        in_specs.append(pl.BlockSpec((tm, tn), lambda i, j: (i, j)))
        args.append(residual)
    return pl.pallas_call(
        functools.partial(_dense_kernel, n_groups=n_groups, has_norm=has_norm, has_res=has_res),
        grid=(m // tm, n // tn),
        in_specs=in_specs,
        out_specs=pl.BlockSpec((tm, tn), lambda i, j: (i, j)),
        out_shape=jax.ShapeDtypeStruct((m, n), jnp.float32),
        scratch_shapes=[pltpu.VMEM((tm, k), jnp.bfloat16)],
        compiler_params=pltpu.CompilerParams(
            dimension_semantics=("parallel", "arbitrary"), vmem_limit_bytes=VMEM_LIMIT),
        name="dense",
    )(*args)


def rmsnorm(x, g):
    xf = x.astype(jnp.float32)
    y = xf * lax.rsqrt(jnp.mean(xf * xf, axis=-1, keepdims=True) + EPS)
    return (y * g.astype(jnp.float32)).astype(x.dtype)


def rope(x, pos, rot):
    half = rot // 2
    inv = ROPE_THETA ** (-2.0 * jnp.arange(half, dtype=jnp.float32) / rot)
    ang = pos.astype(jnp.float32)[:, None] * inv[None, :]
    shape = (pos.shape[0],) + (1,) * (x.ndim - 3) + (half,)
    cos, sin = jnp.cos(ang).reshape(shape), jnp.sin(ang).reshape(shape)
    xf = x.astype(jnp.float32)
    x1, x2 = xf[..., :half], xf[..., half:rot]
    return jnp.concatenate([x1 * cos - x2 * sin, x2 * cos + x1 * sin, xf[..., rot:]], axis=-1).astype(x.dtype)


def masked_softmax(s, mask):
    s = jnp.where(mask, s.astype(jnp.float32), -jnp.inf)
    m = jnp.max(s, axis=-1, keepdims=True)
    e = jnp.exp(s - jnp.where(jnp.isfinite(m), m, 0.0))
    return e / jnp.maximum(jnp.sum(e, axis=-1, keepdims=True), 1e-30)


def project(x, pos, norm_g, w_in_b, q_norm, kv_norm, w_uq, w_uk, gate_bias):
    B, T = x.shape[0], x.shape[1]
    z = dense(x.reshape(B * T, D_MODEL), w_in_b, norm_g=norm_g)[:, :P_IN].reshape(B, T, P_IN)
    idx, acc = [], 0
    for n in IN_SIZES[:-1]:
        acc += n
        idx.append(acc)
    c_q, c_kv, k_pe, q_b, kv_b, g_b, x_c, x_d = jnp.split(z, idx, axis=-1)
    qa = jnp.einsum('btr,rhd->bthd', rmsnorm(c_q, q_norm), w_uq)
    q_abs = jnp.einsum('bthd,rhd->bthr', qa[..., :NOPE_A], w_uk)
    q_pe = rope(qa[..., NOPE_A:], pos, ROPE_A)
    ckv = rmsnorm(c_kv, kv_norm)
    kpe = rope(k_pe, pos, ROPE_A)
    qn = rope(q_b.reshape(B, T, H_B, HD_B), pos, ROT_B)
    kv = kv_b.reshape(B, T, 3, 2, HD_B)
    kv = jnp.stack([rope(kv[:, :, :, 0], pos, ROT_B), kv[:, :, :, 1]], axis=3).reshape(B, T, 6, HD_B)
    gates = jax.nn.sigmoid(g_b + gate_bias).reshape(B, T, H_B, 3)
    return q_abs, q_pe, ckv, kpe, qn, kv, gates, x_c, x_d


def mla_attend(q_abs, q_pe, ckv, kpe, mask):
    s = (jnp.einsum('...thr,...kr->...htk', q_abs, ckv)
         + jnp.einsum('...thp,...kp->...htk', q_pe, kpe)) * MLA_SCALE
    p = masked_softmax(s, mask)
    return jnp.einsum('...htk,...kr->...thr', p.astype(ckv.dtype), ckv)


def mla_prompt(q_abs, q_pe, ckv, kpe):
    B, S = q_abs.shape[0], q_abs.shape[1]
    nb = S // Q_BLOCK
    qb = q_abs.reshape(B, nb, Q_BLOCK, H_A, KV_LORA).swapaxes(0, 1)
    pb = q_pe.reshape(B, nb, Q_BLOCK, H_A, ROPE_A).swapaxes(0, 1)
    k_pos = jnp.arange(S)

    def block(args):
        qa, qp, i = args
        q_pos = i * Q_BLOCK + jnp.arange(Q_BLOCK)
        return mla_attend(qa, qp, ckv, kpe, k_pos[None, :] <= q_pos[:, None])

    o = lax.map(block, (qb, pb, jnp.arange(nb)))
    return o.swapaxes(0, 1).reshape(B, S, H_A, KV_LORA)


def mla_sample(q_abs, q_pe, ckv_new, kpe_new, cache, layer, page_table):
    T = q_abs.shape[1]
    q_pos = PAST_LEN + jnp.arange(T)
    k_pos = jnp.arange(PAST_LEN + T)
    mask = k_pos[None, :] <= q_pos[:, None]

    def one(args):
        qa, qp, cn, kn, pt = args
        rows = cache[layer, pt].reshape(-1, KV_LORA + ROPE_A)
        ckv = jnp.concatenate([rows[:, :KV_LORA].astype(cn.dtype), cn], axis=0)
        kpe = jnp.concatenate([rows[:, KV_LORA:].astype(kn.dtype), kn], axis=0)
        return mla_attend(qa, qp, ckv, kpe, mask)

    return lax.map(one, (q_abs, q_pe, ckv_new, kpe_new, page_table))


def mla_out(o_lat, w_uv):
    B, T = o_lat.shape[0], o_lat.shape[1]
    return jnp.einsum('bthr,rhd->bthd', o_lat, w_uv).reshape(B, T, GROUP_W)


def nsa_compress(k, pe, w1, b1, w2):
    n_cmp = (k.shape[0] - L_CMP) // D_CMP + 1
    idx = (jnp.arange(n_cmp) * D_CMP)[:, None] + jnp.arange(L_CMP)[None, :]
    hid = jax.nn.gelu((k[idx] + pe) @ w1 + b1)
    return hid.reshape(n_cmp, L_CMP * HD_B) @ w2


def compress_kv(k, v, pe, w1, b1, w2):
    return (nsa_compress(k, pe[0], w1[0], b1[0], w2[0]), nsa_compress(v, pe[1], w1[1], b1[1], w2[1]))


def nsa_cmp_sel(q, q_pos, kc, vc, k_sel, v_sel):
    T = q.shape[0]
    n_cmp, n = kc.shape[0], k_sel.shape[0]
    n_blk = -(-n // L_SEL)
    c_end = jnp.arange(n_cmp) * D_CMP + (L_CMP - 1)
    p_c = masked_softmax(jnp.einsum('thd,nd->htn', q, kc) * NSA_SCALE, c_end[None, :] <= q_pos[:, None])
    o_cmp = jnp.einsum('htn,nd->thd', p_c.astype(vc.dtype), vc)
    ci = jnp.arange(n_cmp)[:, None] * D_CMP
    bj = jnp.arange(n_blk)[None, :] * L_SEL
    cover = ((ci < bj + L_SEL) & (ci + L_CMP > bj)).astype(jnp.float32)
    imp = jnp.sum(p_c, axis=0) @ cover
    cur = (q_pos // L_SEL)[:, None]
    j = jnp.arange(n_blk)[None, :]
    causal = j <= cur
    forced = (j == 0) | (j == cur) | (j == cur - 1)
    score = jnp.where(forced, FORCE, jnp.where(causal, imp, -FORCE))
    k_top = min(N_SEL, n_blk)
    _, sel = lax.top_k(score, k_top)
    sel_ok = jnp.take_along_axis(causal, sel, axis=1)
    pos = (sel[:, :, None] * L_SEL + jnp.arange(L_SEL)[None, None, :]).reshape(T, k_top * L_SEL)
    ok = jnp.repeat(sel_ok, L_SEL, axis=1) & (pos <= q_pos[:, None])
    pos_c = jnp.minimum(pos, n - 1)
    kg, vg = k_sel[pos_c], v_sel[pos_c]
    p_s = masked_softmax(jnp.einsum('thd,tkd->htk', q, kg) * NSA_SCALE, ok[None])
    o_sel = jnp.einsum('htk,tkd->thd', p_s.astype(vg.dtype), vg)
    return o_cmp, o_sel


def window_attend(q, q_pos, k, v, k_pos):
    ok = ((k_pos[None, :] <= q_pos[:, None]) & (k_pos[None, :] > q_pos[:, None] - WINDOW)
          & (k_pos[None, :] >= 0))
    p = masked_softmax(jnp.einsum('bthd,bkd->bhtk', q, k) * NSA_SCALE, ok)
    return jnp.einsum('bhtk,bkd->bthd', p.astype(v.dtype), v)


def nsa_prompt(q, kv, pe, w1, b1, w2):
    B, S = q.shape[0], q.shape[1]
    kc, vc = jax.vmap(compress_kv, in_axes=(0, 0, None, None, None, None))(kv[:, :, 0], kv[:, :, 1], pe, w1, b1, w2)
    k_sel, v_sel = kv[:, :, 2], kv[:, :, 3]
    pad = ((0, 0), (WINDOW, 0), (0, 0))
    k_win, v_win = jnp.pad(kv[:, :, 4], pad), jnp.pad(kv[:, :, 5], pad)
    nb = S // Q_BLOCK
    qb = q.reshape(B, nb, Q_BLOCK, H_B, HD_B).swapaxes(0, 1)
    sparse = jax.vmap(nsa_cmp_sel, in_axes=(0, None, 0, 0, 0, 0))

    def block(args):
        qq, i = args
        start = i * Q_BLOCK
        q_pos = start + jnp.arange(Q_BLOCK)
        o_c, o_s = sparse(qq, q_pos, kc, vc, k_sel, v_sel)
        k_pos = start - WINDOW + jnp.arange(WINDOW + Q_BLOCK)
        kb = lax.dynamic_slice_in_dim(k_win, start, WINDOW + Q_BLOCK, axis=1)
        vb = lax.dynamic_slice_in_dim(v_win, start, WINDOW + Q_BLOCK, axis=1)
        return o_c, o_s, window_attend(qq, q_pos, kb, vb, k_pos)

    o_c, o_s, o_w = lax.map(block, (qb, jnp.arange(nb)))
    unblock = lambda o: o.swapaxes(0, 1).reshape(B, S, H_B, HD_B)
    return unblock(o_c), unblock(o_s), unblock(o_w)


def nsa_sample(q, kv, cache, layer, page_table, win_buf, pe, w1, b1, w2):
    T = q.shape[1]
    q_pos = PAST_LEN + jnp.arange(T)

    def one(args):
        qq, kn, pt = args
        rows = cache[layer, pt].reshape(-1, 4, HD_B).astype(kn.dtype)
        full = jnp.concatenate([rows, kn[:, :4]], axis=0)
        kc, vc = compress_kv(full[:, 0], full[:, 1], pe, w1, b1, w2)
        return nsa_cmp_sel(qq, q_pos, kc, vc, full[:, 2], full[:, 3])

    o_c, o_s = lax.map(one, (q, kv, page_table))
    lb = win_buf.shape[1]
    wkv = jnp.concatenate([win_buf.astype(kv.dtype), kv[:, :, 4:6]], axis=1)
    k_pos = PAST_LEN - lb + jnp.arange(lb + T)
    o_w = window_attend(q, q_pos, wkv[:, :, 0], wkv[:, :, 1], k_pos)
    return o_c, o_s, o_w, wkv[:, -lb:]


def nsa_merge(gates, o_c, o_s, o_w):
    B, T = o_c.shape[0], o_c.shape[1]
    o = gates[..., 0:1] * o_c + gates[..., 1:2] * o_s + gates[..., 2:3] * o_w
    return o.reshape(B, T, GROUP_W)


def pool_mix(xs, w, scale):
    B, L = xs.shape[0], xs.shape[1]
    xf = xs.astype(jnp.float32)
    cs0 = jnp.pad(jnp.cumsum(xf, axis=1), ((0, 0), (1, 0), (0, 0)))
    r = jnp.arange(L)
    parts = []
    for g, win in enumerate(POOL_WINDOWS):
        c0, c1 = g * POOL_GW, (g + 1) * POOL_GW
        start = jnp.maximum(r + 1 - win, 0)
        cnt = (r + 1 - start).astype(jnp.float32)[None, :, None]
        mean = (cs0[:, 1:, c0:c1] - cs0[:, start, c0:c1]) / cnt
        parts.append(mean - xf[:, :, c0:c1])
    pooled = jnp.stack(parts, axis=2).astype(xs.dtype)
    y = jnp.einsum('blgc,gcd->blgd', pooled, w).reshape(B, L, GROUP_W)
    return y * scale


def gmlp_mix(zd, v_norm, ws, bs):
    B, L = zd.shape[0], zd.shape[1]
    u, v = jnp.split(jax.nn.gelu(zd), 2, axis=-1)
    v = rmsnorm(v, v_norm)
    n = min(L, CHUNK)
    wm = ws[:, :n, :n] * jnp.tril(jnp.ones((n, n), ws.dtype))
    vc = v.reshape(B, L // n, n, G_D, GD_W)
    s = jnp.einsum('gts,bcsgd->bctgd', wm, vc) + bs[:, :n].T[None, None, :, :, None]
    return u * s.reshape(B, L, GROUP_W), v


def mem_attend(x, norm_g, mkv, wq_b, wo_b):
    B, T = x.shape[0], x.shape[1]
    x2 = x.reshape(B * T, D_MODEL)
    q = dense(x2, wq_b, norm_g=norm_g).reshape(B, T, H_M, HD_M)
    k = mkv[:, :, 0].reshape(B, -1, H_M, HD_M)
    v = mkv[:, :, 1].reshape(B, -1, H_M, HD_M)
    p = jax.nn.softmax(jnp.einsum('bthd,bmhd->bhtm', q, k).astype(jnp.float32) * MEM_SCALE, axis=-1)
    o = jnp.einsum('bhtm,bmhd->bthd', p.astype(v.dtype), v).reshape(B * T, H_M * HD_M)
    return dense(o, wo_b, residual=x2).reshape(B, T, D_MODEL)


def conv_ffn(x, norm_g, buf, w_up_b, conv_w, conv_b, w_down_b):
    B, L = x.shape[0], x.shape[1]
    x2 = x.reshape(B * L, D_MODEL)
    a = dense(x2, w_up_b, norm_g=norm_g).reshape(B, L, 2 * D_FF)
    ap = jnp.concatenate([buf.astype(a.dtype), a], axis=1)
    c = conv_b
    for k in range(CONV_W):
        c = c + ap[:, k:k + L] * conv_w[k]
    g, u = jnp.split(c, 2, axis=-1)
    act = (jax.nn.silu(g) * u).reshape(B * L, D_FF)
    y = dense(act, w_down_b, residual=x2, tm=256, tn_cap=512)
    return y.reshape(B, L, D_MODEL), ap[:, L:]


def kernel(x_prompt, x_sample, cache_mla, cache_nsa, state_nsa_win, state_pool, state_conv, cache_mem_kv,
           page_table, mem_prompt, mix_norm, w_in, mla_q_norm, mla_kv_norm, mla_w_uq, mla_w_ukv,
           nsa_gate_bias, nsa_cmp_pe, nsa_cmp_w1, nsa_cmp_b1, nsa_cmp_w2, pool_w, pool_scale,
           gmlp_v_norm, gmlp_ws, gmlp_bs, mix_out_norm, w_out, mem_norm, mem_kv_norm, mem_wq, mem_wkv,
           mem_wo, ffn_norm, ffn_w_up, ffn_conv_w, ffn_conv_b, ffn_w_down, final_norm):
    B, S = x_prompt.shape[0], x_prompt.shape[1]
    DB, T = x_sample.shape[0], x_sample.shape[1]
    pos_p = jnp.arange(S, dtype=jnp.int32)
    pos_s = PAST_LEN + jnp.arange(T, dtype=jnp.int32)
    lb_p = min(WINDOW, S)
    xp, xs = x_prompt, x_sample
    bf = jnp.bfloat16
    mla_p, mla_s, nsa_p, nsa_s, win_p, win_s = [], [], [], [], [], []
    pool_p, pool_s, gv_s, conv_p, conv_s, memkv_p = [], [], [], [], [], []
    for l in range(DEPTH):
        w_uk = mla_w_ukv[l][..., :NOPE_A]
        w_uv = mla_w_ukv[l][..., NOPE_A:]
        cmp = (nsa_cmp_pe[l], nsa_cmp_w1[l], nsa_cmp_b1[l], nsa_cmp_w2[l])
        w_in_b = jnp.pad(w_in[l], ((0, 0), (0, P_IN_PAD - P_IN))).astype(bf)
        w_out_b = w_out[l].astype(bf)

        def mix_merge(x, oa, ob, oc, od):
            Bx, Tx = x.shape[0], x.shape[1]
            o = jnp.concatenate([oa, ob, oc, od], axis=-1).reshape(Bx * Tx, D_MODEL)
            return dense(o, w_out_b, norm_g=mix_out_norm[l], n_groups=4,
                         residual=x.reshape(Bx * Tx, D_MODEL)).reshape(Bx, Tx, D_MODEL)

        qa, qpe, ckv, kpe, qb, kv, gt, xc, xd = project(xp, pos_p, mix_norm[l], w_in_b, mla_q_norm[l],
                                                        mla_kv_norm[l], mla_w_uq[l], w_uk, nsa_gate_bias[l])
        oa = mla_out(mla_prompt(qa, qpe, ckv, kpe), w_uv)
        o_c, o_s, o_w = nsa_prompt(qb, kv, *cmp)
        ob = nsa_merge(gt, o_c, o_s, o_w)
        oc = pool_mix(xc, pool_w[l], pool_scale[l])
        od, _ = gmlp_mix(xd, gmlp_v_norm[l], gmlp_ws[l], gmlp_bs[l])
        xp = mix_merge(xp, oa, ob, oc, od)
        mla_p.append(jnp.concatenate([ckv, kpe], axis=-1))
        nsa_p.append(kv[:, :, :4])
        win_p.append(kv[:, S - lb_p:, 4:6])
        pool_p.append(xc[:, S - POOL_BUF:])

        qa, qpe, ckv, kpe, qb, kv, gt, xc, xd = project(xs, pos_s, mix_norm[l], w_in_b, mla_q_norm[l],
                                                        mla_kv_norm[l], mla_w_uq[l], w_uk, nsa_gate_bias[l])
        oa = mla_out(mla_sample(qa, qpe, ckv, kpe, cache_mla, l, page_table), w_uv)
        o_c, o_s, o_w, new_win = nsa_sample(qb, kv, cache_nsa, l, page_table, state_nsa_win[l], *cmp)
        ob = nsa_merge(gt, o_c, o_s, o_w)
        xcat = jnp.concatenate([state_pool[l].astype(xc.dtype), xc], axis=1)
        oc = pool_mix(xcat, pool_w[l], pool_scale[l])[:, POOL_BUF:]
        od, v_rows = gmlp_mix(xd, gmlp_v_norm[l], gmlp_ws[l], gmlp_bs[l])
        xs = mix_merge(xs, oa, ob, oc, od)
        mla_s.append(jnp.concatenate([ckv, kpe], axis=-1))
        nsa_s.append(kv[:, :, :4])
        win_s.append(new_win)
        pool_s.append(xcat[:, xcat.shape[1] - POOL_BUF:])
        gv_s.append(v_rows)

        wq_b, wo_b = mem_wq[l].astype(bf), mem_wo[l].astype(bf)
        mkv = dense(mem_prompt.reshape(B * MEM_LEN, D_MODEL), mem_wkv[l].astype(bf),
                    norm_g=mem_kv_norm[l]).reshape(B, MEM_LEN, 2, H_M * HD_M)
        xp = mem_attend(xp, mem_norm[l], mkv, wq_b, wo_b)
        xs = mem_attend(xs, mem_norm[l], cache_mem_kv[l].astype(xs.dtype), wq_b, wo_b)
        memkv_p.append(mkv)

        w_up_b, w_down_b = ffn_w_up[l].astype(bf), ffn_w_down[l].astype(bf)
        xp, cbp = conv_ffn(xp, ffn_norm[l], jnp.zeros((B, CONV_W - 1, 2 * D_FF), xp.dtype), w_up_b,
                           ffn_conv_w[l], ffn_conv_b[l], w_down_b)
        xs, cbs = conv_ffn(xs, ffn_norm[l], state_conv[l], w_up_b, ffn_conv_w[l], ffn_conv_b[l], w_down_b)
        conv_p.append(cbp)
        conv_s.append(cbs)

    y_prompt = rmsnorm(xp, final_norm)
    y_sample = rmsnorm(xs, final_norm)
    return (y_prompt, y_sample, jnp.stack(mla_p), jnp.stack(mla_s), jnp.stack(nsa_p), jnp.stack(nsa_s),
            jnp.stack(win_p), jnp.stack(win_s), jnp.stack(pool_p), jnp.stack(pool_s), jnp.stack(gv_s),
            jnp.stack(conv_p), jnp.stack(conv_s), jnp.stack(memkv_p))
```

```python
import functools

import jax
import jax.numpy as jnp
import numpy as np
from jax import lax
from jax.experimental import pallas as pl
from jax.experimental.pallas import tpu as pltpu

D_MODEL = 2048
DEPTH = 2
PAST_LEN = 16384
PAGE_SIZE = 128
GROUP_W = 512
H_A = 4
NOPE_A = 128
ROPE_A = 64
V_A = 128
Q_LORA = 384
KV_LORA = 256
MLA_SCALE = (NOPE_A + ROPE_A) ** -0.5
H_B = 8
HD_B = 64
ROT_B = 16
L_CMP = 32
D_CMP = 16
L_SEL = 64
N_SEL = 16
WINDOW = 512
NSA_SCALE = HD_B ** -0.5
FORCE = 1e9
POOL_WINDOWS = (2, 4, 8, 16)
POOL_GW = 128
POOL_BUF = 15
CHUNK = 128
G_D = 4
GD_W = 128
MEM_LEN = 256
H_M = 4
HD_M = 128
MEM_SCALE = HD_M ** -0.5
D_FF = 5632
CONV_W = 3
ROPE_THETA = 500000.0
EPS = 1e-6
Q_BLOCK = 128
IN_SIZES = (Q_LORA, KV_LORA, ROPE_A, H_B * HD_B, 6 * HD_B, 3 * H_B, GROUP_W, 2 * GROUP_W)
P_IN = sum(IN_SIZES)
P_IN_PAD = 3200

V7X_VMEM_BYTES = 64 * 1024 * 1024
LANE = 128
VMEM_LIMIT = 56 * 1024 * 1024


def _dense_kernel(*refs, n_groups, has_norm, has_res):
    it = iter(refs)
    x_ref = next(it)
    g_ref = next(it) if has_norm else None
    w_ref = next(it)
    r_ref = next(it) if has_res else None
    o_ref = next(it)
    xn_ref = next(it)

    @pl.when(pl.program_id(1) == 0)
    def _():
        x = x_ref[...]
        if has_norm:
            xf = x.astype(jnp.float32)
            gw = xf.shape[-1] // n_groups
            parts = []
            for gi in range(n_groups):
                xg = xf[:, gi * gw:(gi + 1) * gw]
                ms = jnp.mean(xg * xg, axis=-1, keepdims=True)
                parts.append(xg * lax.rsqrt(ms + EPS) * g_ref[:, gi * gw:(gi + 1) * gw])
            xf = parts[0] if n_groups == 1 else jnp.concatenate(parts, axis=-1)
            xn_ref[...] = xf.astype(jnp.bfloat16)
        else:
            xn_ref[...] = x.astype(jnp.bfloat16)

    acc = jnp.dot(xn_ref[...], w_ref[...].astype(jnp.bfloat16), preferred_element_type=jnp.float32)
    if has_res:
        acc = acc + r_ref[...]
    o_ref[...] = acc


def _pick_tile(n, cap):
    best = None
    for t in range(LANE, min(n, cap) + 1, LANE):
        if n % t == 0:
            best = t
    return best if best is not None else n


def dense(x, w, layer=None, norm_g=None, n_groups=1, residual=None, tm=1024, tn_cap=512):
    m, k = x.shape
    n = w.shape[-1]
    tm = min(tm, m)
    assert m % tm == 0
    tn = _pick_tile(n, tn_cap)
    has_norm = norm_g is not None
    has_res = residual is not None
    in_specs = [pl.BlockSpec((tm, k), lambda i, j: (i, 0))]
    args = [x]
    if has_norm:
        in_specs.append(pl.BlockSpec((1, k), lambda i, j: (0, 0)))
        args.append(norm_g.reshape(1, k).astype(jnp.float32))
    if layer is None:
        in_specs.append(pl.BlockSpec((k, tn), lambda i, j: (0, j)))
    else:
        in_specs.append(pl.BlockSpec((None, k, tn), lambda i, j: (layer, 0, j)))
    args.append(w)
    if has_res:
        in_specs.append(pl.BlockSpec((tm, tn), lambda i, j: (i, j)))
        args.append(residual)
    return pl.pallas_call(
        functools.partial(_dense_kernel, n_groups=n_groups, has_norm=has_norm, has_res=has_res),
        grid=(m // tm, n // tn),
        in_specs=in_specs,
        out_specs=pl.BlockSpec((tm, tn), lambda i, j: (i, j)),
        out_shape=jax.ShapeDtypeStruct((m, n), jnp.float32),
        scratch_shapes=[pltpu.VMEM((tm, k), jnp.bfloat16)],
        compiler_params=pltpu.CompilerParams(
            dimension_semantics=("parallel", "arbitrary"), vmem_limit_bytes=VMEM_LIMIT),
        name="dense",
    )(*args)


NSA_TQ = 128
NSA_TK = 512
NEG = -1e30
_NT = (((1,), (1,)), ((), ()))


def _softmax_rows(s3, mask2):
    sm = jnp.where(mask2[None], s3, NEG)
    m = jnp.max(sm, axis=-1, keepdims=True)
    e = jnp.where(mask2[None], jnp.exp(sm - m), 0.0)
    den = jnp.sum(e, axis=-1, keepdims=True)
    return e * (1.0 / jnp.maximum(den, 1e-30))


def _nsa_prompt_kernel(q_ref, sel_ref, win_ref, kcvc_ref, g_ref, e_ref, cover_ref, o_ref,
                       m_ref, l_ref, acc_ref, *, n_cmp, n_blk):
    f32, bf = jnp.float32, jnp.bfloat16
    rows = H_B * NSA_TQ
    s0 = pl.program_id(1) * NSA_TQ
    q = q_ref[...].reshape(rows, 2 * HD_B).astype(bf)
    qpos = s0 + lax.broadcasted_iota(jnp.int32, (NSA_TQ, 1), 0)

    n_pad = kcvc_ref.shape[0]
    kcvc = kcvc_ref[...].astype(bf)
    sc = lax.dot_general(q, kcvc, _NT, preferred_element_type=f32) * NSA_SCALE
    n_idx = lax.broadcasted_iota(jnp.int32, (NSA_TQ, n_pad), 1)
    cmask = (n_idx * D_CMP + (L_CMP - 1) <= qpos) & (n_idx < n_cmp)
    p_c = _softmax_rows(sc.reshape(H_B, NSA_TQ, n_pad), cmask)
    o_c = jnp.dot(p_c.reshape(rows, n_pad).astype(bf), kcvc, preferred_element_type=f32)

    psum = jnp.sum(p_c, axis=0)
    p_hi = psum.astype(bf)
    p_lo = (psum - p_hi.astype(f32)).astype(bf)
    cover = cover_ref[...]
    imp = (jnp.dot(p_hi, cover, preferred_element_type=f32)
           + jnp.dot(p_lo, cover, preferred_element_type=f32))
    j = lax.broadcasted_iota(jnp.int32, (NSA_TQ, n_blk), 1)
    cur = qpos // L_SEL
    causal = j <= cur
    forced = (j == 0) | (j == cur) | (j == cur - 1)
    score = jnp.where(forced, FORCE, jnp.where(causal, imp, -FORCE))
    rank = jnp.zeros((NSA_TQ, n_blk), f32)
    for i in range(n_blk):
        col = score[:, i:i + 1]
        beats = (col > score) | ((col == score) & (j > i))
        rank = rank + jnp.where(beats, 1.0, 0.0)
    selb = jnp.where((rank < float(min(N_SEL, n_blk))) & causal, 1.0, 0.0).astype(bf)

    m_ref[...] = jnp.full(m_ref.shape, NEG, f32)
    l_ref[...] = jnp.zeros(l_ref.shape, f32)
    acc_ref[...] = jnp.zeros(acc_ref.shape, f32)

    def body(k, carry):
        kv = sel_ref[pl.ds(pl.multiple_of(k * NSA_TK, NSA_TK), NSA_TK), :].astype(bf)
        s = lax.dot_general(q, kv, _NT, preferred_element_type=f32) * NSA_SCALE
        em = jnp.dot(selb, e_ref[k], preferred_element_type=f32)
        kpos = k * NSA_TK + lax.broadcasted_iota(jnp.int32, (1, NSA_TK), 1)
        mask = ((em > 0.5) & (kpos <= qpos))[None]
        s3 = jnp.where(mask, s.reshape(H_B, NSA_TQ, NSA_TK), NEG)
        m_old = m_ref[...].reshape(H_B, NSA_TQ, 1)
        m_new = jnp.maximum(m_old, jnp.max(s3, axis=-1, keepdims=True))
        alpha = jnp.exp(m_old - m_new)
        p = jnp.where(mask, jnp.exp(s3 - m_new), 0.0)
        l_old = l_ref[...].reshape(H_B, NSA_TQ, 1)
        l_ref[...] = (alpha * l_old + jnp.sum(p, axis=-1, keepdims=True)).reshape(rows, 1)
        pv = jnp.dot(p.reshape(rows, NSA_TK).astype(bf), kv, preferred_element_type=f32)
        acc_ref[...] = alpha.reshape(rows, 1) * acc_ref[...] + pv
        m_ref[...] = m_new.reshape(rows, 1)
        return carry

    lax.fori_loop(0, (s0 + NSA_TQ + NSA_TK - 1) // NSA_TK, body, 0)
    o_s = acc_ref[...] * (1.0 / jnp.maximum(l_ref[...], 1e-30))

    w0 = pl.multiple_of(jnp.maximum(s0 - WINDOW, 0), NSA_TQ)
    kvw = win_ref[pl.ds(w0, WINDOW + NSA_TQ), :].astype(bf)
    sw = lax.dot_general(q, kvw, _NT, preferred_element_type=f32) * NSA_SCALE
    kpos = w0 + lax.broadcasted_iota(jnp.int32, (1, WINDOW + NSA_TQ), 1)
    wmask = (kpos <= qpos) & (kpos > qpos - WINDOW)
    p_w = _softmax_rows(sw.reshape(H_B, NSA_TQ, WINDOW + NSA_TQ), wmask)
    o_w = jnp.dot(p_w.reshape(rows, WINDOW + NSA_TQ).astype(bf), kvw, preferred_element_type=f32)

    g = g_ref[...].reshape(rows, 3)
    o = (g[:, 0:1] * o_c + g[:, 1:2] * o_s + g[:, 2:3] * o_w).reshape(H_B, NSA_TQ, 2 * HD_B)
    lane = lax.broadcasted_iota(jnp.int32, (NSA_TQ, 2 * HD_B), 1)
    for hp in range(H_B // 2):
        lo = pltpu.roll(o[2 * hp], HD_B, 1)
        o_ref[:, hp * 2 * HD_B:(hp + 1) * 2 * HD_B] = jnp.where(lane < HD_B, lo, o[2 * hp + 1])


def nsa_prompt_attend(qn, kv, kc, vc, gates):
    B, S = qn.shape[0], qn.shape[1]
    assert S % NSA_TK == 0 and S >= WINDOW + NSA_TQ
    n_cmp = kc.shape[1]
    n_pad = -(-n_cmp // LANE) * LANE
    n_blk = S // L_SEL
    q_pad = jnp.pad(qn.transpose(0, 2, 1, 3), ((0, 0), (0, 0), (0, 0), (0, HD_B)))
    kv3 = kv.reshape(B, S, 6 * HD_B)
    kcvc = jnp.pad(jnp.concatenate([kc, vc], axis=-1), ((0, 0), (0, n_pad - n_cmp), (0, 0)))
    g_t = gates.transpose(0, 2, 1, 3)
    blk_of_key = np.arange(S) // L_SEL
    expand = (np.arange(n_blk)[:, None] == blk_of_key[None, :]).astype(np.float32)
    e3 = jnp.asarray(expand.reshape(n_blk, S // NSA_TK, NSA_TK).transpose(1, 0, 2), jnp.bfloat16)
    ci = np.arange(n_pad)[:, None] * D_CMP
    bj = np.arange(n_blk)[None, :] * L_SEL
    cover = jnp.asarray(((ci < bj + L_SEL) & (ci + L_CMP > bj)).astype(np.float32), jnp.bfloat16)
    rows = H_B * NSA_TQ
    return pl.pallas_call(
        functools.partial(_nsa_prompt_kernel, n_cmp=n_cmp, n_blk=n_blk),
        grid=(B, S // NSA_TQ),
        in_specs=[
            pl.BlockSpec((None, H_B, NSA_TQ, 2 * HD_B), lambda b, i: (b, 0, i, 0)),
            pl.BlockSpec((None, S, 2 * HD_B), lambda b, i: (b, 0, 1)),
            pl.BlockSpec((None, S, 2 * HD_B), lambda b, i: (b, 0, 2)),
            pl.BlockSpec((None, n_pad, 2 * HD_B), lambda b, i: (b, 0, 0)),
            pl.BlockSpec((None, H_B, NSA_TQ, 3), lambda b, i: (b, 0, i, 0)),
            pl.BlockSpec((S // NSA_TK, n_blk, NSA_TK), lambda b, i: (0, 0, 0)),
            pl.BlockSpec((n_pad, n_blk), lambda b, i: (0, 0)),
        ],
        out_specs=pl.BlockSpec((None, NSA_TQ, GROUP_W), lambda b, i: (b, i, 0)),
        out_shape=jax.ShapeDtypeStruct((B, S, GROUP_W), jnp.float32),
        scratch_shapes=[pltpu.VMEM((rows, 1), jnp.float32), pltpu.VMEM((rows, 1), jnp.float32),
                        pltpu.VMEM((rows, 2 * HD_B), jnp.float32)],
        compiler_params=pltpu.CompilerParams(
            dimension_semantics=("parallel", "arbitrary"), vmem_limit_bytes=VMEM_LIMIT),
        name="nsa_prompt",
    )(q_pad, kv3, kv3, kcvc, g_t, e3, cover)


NSA_NP = 32
NSA_TKS = 2048
GROUP_ROWS = D_CMP
assert L_CMP == 2 * D_CMP and PAGE_SIZE % GROUP_ROWS == 0


def _nsa_sample_kernel(pt_ref, *refs, n_pages, t_new):
    del pt_ref
    f32, bf = jnp.float32, jnp.bfloat16
    page_refs = refs[:NSA_NP]
    (q_ref, new_ref, win_ref, g_ref, pe_ref, bd1_ref, b1_ref, w2p_ref, cover_ref, eloc_ref,
     o_ref, ab_ref, ksvs_ref) = refs[NSA_NP:]
    c = pl.program_id(1)
    gpp = PAGE_SIZE // GROUP_ROWS
    gpc = NSA_NP * gpp
    n_grp = n_pages * gpp
    n_past = n_pages * PAGE_SIZE
    n_blk = n_past // L_SEL
    rows = H_B * t_new

    for k in range(NSA_NP):
        row0 = pl.multiple_of((c * NSA_NP + k) * PAGE_SIZE, PAGE_SIZE)
        ksvs_ref[pl.ds(row0, PAGE_SIZE), :] = page_refs[k][pl.ds(1, PAGE_SIZE, stride=2), :].astype(bf)

    ab = jnp.zeros((gpc, 4 * HD_B), f32)
    for p in range(GROUP_ROWS):
        xp = jnp.concatenate(
            [page_refs[k][pl.ds(2 * p, gpp, stride=2 * GROUP_ROWS), :] for k in range(NSA_NP)], axis=0)
        x2 = jnp.concatenate([xp, xp], axis=1) + pe_ref[p:p + 1, :]
        h = jax.nn.gelu(jnp.dot(x2.astype(bf), bd1_ref[...], preferred_element_type=f32) + b1_ref[...])
        ab = ab + jnp.dot(h.astype(bf), w2p_ref[p], preferred_element_type=f32)
    ab_ref[pl.ds(pl.multiple_of(c * gpc, gpc), gpc), :] = ab

    @pl.when(c == pl.num_programs(1) - 1)
    def _():
        q = q_ref[...].astype(bf)
        tq = lax.broadcasted_iota(jnp.int32, (t_new, 1), 0)

        abv = ab_ref[...]
        kcvc = (abv[:, 0:2 * HD_B] + pltpu.roll(abv[:, 2 * HD_B:4 * HD_B], n_grp - 1, 0)).astype(bf)
        sc = lax.dot_general(q, kcvc, _NT, preferred_element_type=f32) * NSA_SCALE
        cmask = lax.broadcasted_iota(jnp.int32, (t_new, n_grp), 1) < n_grp - 1
        p_c = _softmax_rows(sc.reshape(H_B, t_new, n_grp), cmask)
        o_c = jnp.dot(p_c.reshape(rows, n_grp).astype(bf), kcvc, preferred_element_type=f32)

        psum = jnp.sum(p_c, axis=0)
        p_hi = psum.astype(bf)
        p_lo = (psum - p_hi.astype(f32)).astype(bf)
        cover = cover_ref[...]
        imp = (jnp.dot(p_hi, cover, preferred_element_type=f32)
               + jnp.dot(p_lo, cover, preferred_element_type=f32))
        j = lax.broadcasted_iota(jnp.int32, (t_new, n_blk), 1)
        score = jnp.where((j == 0) | (j == n_blk - 1), FORCE, imp)
        rank = jnp.zeros((t_new, n_blk), f32)
        for i in range(n_blk):
            col = score[:, i:i + 1]
            beats = (col > score) | ((col == score) & (j > i))
            rank = rank + jnp.where(beats, 1.0, 0.0)
        sel = jnp.where(rank < float(N_SEL - 1), 1.0, 0.0)

        m = jnp.full((H_B, t_new, 1), NEG, f32)
        l = jnp.zeros((H_B, t_new, 1), f32)
        acc = jnp.zeros((rows, 2 * HD_B), f32)
        bpt = NSA_TKS // L_SEL

        def fold(m, l, acc, s, mask, kv):
            s3 = jnp.where(mask, s.reshape(H_B, t_new, s.shape[-1]), NEG)
            m_new = jnp.maximum(m, jnp.max(s3, axis=-1, keepdims=True))
            alpha = jnp.exp(m - m_new)
            p = jnp.where(mask, jnp.exp(s3 - m_new), 0.0)
            l = alpha * l + jnp.sum(p, axis=-1, keepdims=True)
            pv = jnp.dot(p.reshape(rows, s.shape[-1]).astype(bf), kv, preferred_element_type=f32)
            return m_new, l, alpha.reshape(rows, 1) * acc + pv

        for k in range(n_past // NSA_TKS):
            kv = ksvs_ref[k * NSA_TKS:(k + 1) * NSA_TKS, :]
            s = lax.dot_general(q, kv, _NT, preferred_element_type=f32) * NSA_SCALE
            em = jnp.dot(sel[:, k * bpt:(k + 1) * bpt], eloc_ref[...], preferred_element_type=f32)
            m, l, acc = fold(m, l, acc, s, (em > 0.5)[None], kv)

        new = new_ref[...]
        pad = jnp.zeros((LANE - t_new, 2 * HD_B), f32)
        kv = jnp.concatenate([new[:, 2 * HD_B:4 * HD_B], pad], axis=0).astype(bf)
        s = lax.dot_general(q, kv, _NT, preferred_element_type=f32) * NSA_SCALE
        kidx = lax.broadcasted_iota(jnp.int32, (1, LANE), 1)
        m, l, acc = fold(m, l, acc, s, (kidx <= tq)[None], kv)
        o_s = acc * (1.0 / jnp.maximum(l, 1e-30)).reshape(rows, 1)

        lb = win_ref.shape[0]
        kvw = jnp.concatenate([win_ref[...], new[:, 4 * HD_B:6 * HD_B], pad], axis=0).astype(bf)
        sw = lax.dot_general(q, kvw, _NT, preferred_element_type=f32) * NSA_SCALE
        kidx = lax.broadcasted_iota(jnp.int32, (1, lb + LANE), 1)
        wmask = (kidx <= lb + tq) & (kidx > lb + tq - WINDOW)
        p_w = _softmax_rows(sw.reshape(H_B, t_new, lb + LANE), wmask)
        o_w = jnp.dot(p_w.reshape(rows, lb + LANE).astype(bf), kvw, preferred_element_type=f32)

        g = g_ref[...]
        o = (g[:, 0:1] * o_c + g[:, 1:2] * o_s + g[:, 2:3] * o_w).reshape(H_B, t_new, 2 * HD_B)
        lane = lax.broadcasted_iota(jnp.int32, (t_new, 2 * HD_B), 1)
        for hp in range(H_B // 2):
            lo = pltpu.roll(o[2 * hp], HD_B, 1)
            o_ref[:, hp * 2 * HD_B:(hp + 1) * 2 * HD_B] = jnp.where(lane < HD_B, lo, o[2 * hp + 1])


def _blockdiag(blocks):
    n = len(blocks)
    r, c = blocks[0].shape
    out = jnp.zeros((n * r, n * c), blocks[0].dtype)
    for i, b in enumerate(blocks):
        out = out.at[i * r:(i + 1) * r, i * c:(i + 1) * c].set(b)
    return out


def nsa_sample_attend(qn, kv_new, gates, cache, layer, page_table, win_buf, pe, w1, b1, w2):
    DB, T = qn.shape[0], qn.shape[1]
    n_pages = page_table.shape[1]
    n_past = n_pages * PAGE_SIZE
    lb = win_buf.shape[1]
    assert n_pages % NSA_NP == 0 and n_past % NSA_TKS == 0 and n_past % L_SEL == 0 and T <= L_SEL
    assert T % 8 == 0 and lb == WINDOW and n_past // L_SEL + 1 >= N_SEL
    assert (n_past + T - L_CMP) // D_CMP + 1 == n_past // D_CMP - 1
    n_grp = n_past // GROUP_ROWS
    n_blk = n_past // L_SEL
    bf = jnp.bfloat16
    rows = H_B * T
    q_pad = jnp.pad(qn.transpose(0, 2, 1, 3).reshape(DB, rows, HD_B), ((0, 0), (0, 0), (0, HD_B)))
    g_t = gates.transpose(0, 2, 1, 3).reshape(DB, rows, 3)
    new3 = kv_new.reshape(DB, T, 6 * HD_B)
    win3 = win_buf.reshape(DB, lb, 2 * HD_B)
    pages = cache.reshape(cache.shape[0] * cache.shape[1], 2 * PAGE_SIZE, 2 * HD_B)
    page_ids = page_table + layer * cache.shape[1]
    pe_kv = jnp.concatenate([pe[0], pe[1]], axis=-1)
    pe2 = jnp.concatenate([pe_kv[:GROUP_ROWS], pe_kv[GROUP_ROWS:]], axis=-1)
    bd1 = _blockdiag([w1[0], w1[1], w1[0], w1[1]]).astype(bf)
    b1x = jnp.concatenate([b1[0], b1[1], b1[0], b1[1]]).reshape(1, 4 * HD_B)
    w2r = w2.reshape(2, L_CMP, HD_B, HD_B)
    w2p = jnp.stack([_blockdiag([w2r[0, p], w2r[1, p], w2r[0, GROUP_ROWS + p], w2r[1, GROUP_ROWS + p]])
                     for p in range(GROUP_ROWS)]).astype(bf)
    ci = np.arange(n_grp)[:, None] * D_CMP
    bj = np.arange(n_blk)[None, :] * L_SEL
    cover = jnp.asarray(((ci < bj + L_SEL) & (ci + L_CMP > bj) & (ci < (n_grp - 1) * D_CMP)).astype(np.float32), bf)
    eloc = jnp.asarray((np.arange(NSA_TKS // L_SEL)[:, None] == (np.arange(NSA_TKS) // L_SEL)[None, :])
                       .astype(np.float32))

    def page_spec(k):
        return pl.BlockSpec((None, 2 * PAGE_SIZE, 2 * HD_B), lambda b, c, pt: (pt[b, c * NSA_NP + k], 0, 0))

    const2 = lambda b, c, pt: (0, 0)
    in_specs = [page_spec(k) for k in range(NSA_NP)] + [
        pl.BlockSpec((None, rows, 2 * HD_B), lambda b, c, pt: (b, 0, 0)),
        pl.BlockSpec((None, T, 6 * HD_B), lambda b, c, pt: (b, 0, 0)),
        pl.BlockSpec((None, lb, 2 * HD_B), lambda b, c, pt: (b, 0, 0)),
        pl.BlockSpec((None, rows, 3), lambda b, c, pt: (b, 0, 0)),
        pl.BlockSpec((GROUP_ROWS, 4 * HD_B), const2),
        pl.BlockSpec((4 * HD_B, 4 * HD_B), const2),
        pl.BlockSpec((1, 4 * HD_B), const2),
        pl.BlockSpec((GROUP_ROWS, 4 * HD_B, 4 * HD_B), lambda b, c, pt: (0, 0, 0)),
        pl.BlockSpec((n_grp, n_blk), const2),
        pl.BlockSpec((NSA_TKS // L_SEL, NSA_TKS), const2),
    ]
    grid_spec = pltpu.PrefetchScalarGridSpec(
        num_scalar_prefetch=1,
        grid=(DB, n_pages // NSA_NP),
        in_specs=in_specs,
        out_specs=pl.BlockSpec((None, T, GROUP_W), lambda b, c, pt: (b, 0, 0)),
        scratch_shapes=[pltpu.VMEM((n_grp, 4 * HD_B), jnp.float32), pltpu.VMEM((n_past, 2 * HD_B), bf)],
    )
    return pl.pallas_call(
        functools.partial(_nsa_sample_kernel, n_pages=n_pages, t_new=T),
        grid_spec=grid_spec,
        out_shape=jax.ShapeDtypeStruct((DB, T, GROUP_W), jnp.float32),
        compiler_params=pltpu.CompilerParams(
            dimension_semantics=("parallel", "arbitrary"), vmem_limit_bytes=VMEM_LIMIT),
        name="nsa_sample",
    )(page_ids, *([pages] * NSA_NP), q_pad, new3, win3, g_t, pe2, bd1, b1x, w2p, cover, eloc)


MLA_NP = 32


def _mla_sample_kernel(pt_ref, *refs, t_new):
    del pt_ref
    f32, bf = jnp.float32, jnp.bfloat16
    page_refs = refs[:MLA_NP]
    qa_ref, qp_ref, new_ref, wuv_ref, o_ref, m_ref, l_ref, acc_ref = refs[MLA_NP:]
    c = pl.program_id(1)
    rows = H_A * t_new

    @pl.when(c == 0)
    def _():
        m_ref[...] = jnp.full(m_ref.shape, NEG, f32)
        l_ref[...] = jnp.zeros(l_ref.shape, f32)
        acc_ref[...] = jnp.zeros(acc_ref.shape, f32)

    qa = qa_ref[...].astype(bf)
    qp = qp_ref[...].astype(bf)

    def fold(kc, kp, mask):
        s = (lax.dot_general(qa, kc, _NT, preferred_element_type=f32)
             + lax.dot_general(qp, kp, _NT, preferred_element_type=f32)) * MLA_SCALE
        if mask is not None:
            s = jnp.where(mask, s, NEG)
        m_old = m_ref[...]
        m_new = jnp.maximum(m_old, jnp.max(s, axis=-1, keepdims=True))
        alpha = jnp.exp(m_old - m_new)
        p = jnp.exp(s - m_new)
        if mask is not None:
            p = jnp.where(mask, p, 0.0)
        l_ref[...] = alpha * l_ref[...] + jnp.sum(p, axis=-1, keepdims=True)
        acc_ref[...] = alpha * acc_ref[...] + jnp.dot(p.astype(bf), kc, preferred_element_type=f32)
        m_ref[...] = m_new

    kc = jnp.concatenate([page_refs[k][:, 0:KV_LORA] for k in range(MLA_NP)], axis=0).astype(bf)
    kp = jnp.concatenate([page_refs[k][:, KV_LORA:KV_LORA + ROPE_A] for k in range(MLA_NP)], axis=0).astype(bf)
    fold(kc, kp, None)

    @pl.when(c == pl.num_programs(1) - 1)
    def _():
        new = jnp.concatenate([new_ref[...], jnp.zeros((LANE - t_new, KV_LORA + ROPE_A), f32)], axis=0)
        tq = lax.broadcasted_iota(jnp.int32, (H_A, t_new, 1), 1).reshape(rows, 1)
        kidx = lax.broadcasted_iota(jnp.int32, (1, LANE), 1)
        fold(new[:, 0:KV_LORA].astype(bf), new[:, KV_LORA:KV_LORA + ROPE_A].astype(bf), kidx <= tq)
        o = (acc_ref[...] * (1.0 / jnp.maximum(l_ref[...], 1e-30))).astype(bf)
        for h in range(H_A):
            o_ref[:, h * V_A:(h + 1) * V_A] = jnp.dot(o[h * t_new:(h + 1) * t_new], wuv_ref[h].astype(bf),
                                                      preferred_element_type=f32)


def mla_sample_attend(q_abs, q_pe, rows_new, cache, layer, page_table, w_uv):
    DB, T = q_abs.shape[0], q_abs.shape[1]
    n_pages = page_table.shape[1]
    assert n_pages % MLA_NP == 0 and T % 8 == 0 and T <= LANE
    rows = H_A * T
    dk = KV_LORA + ROPE_A
    qa = q_abs.transpose(0, 2, 1, 3).reshape(DB, rows, KV_LORA)
    qp = q_pe.transpose(0, 2, 1, 3).reshape(DB, rows, ROPE_A)
    pages = cache.reshape(cache.shape[0] * cache.shape[1], PAGE_SIZE, dk)
    page_ids = page_table + layer * cache.shape[1]
    wuv = w_uv.transpose(1, 0, 2)

    def page_spec(k):
        return pl.BlockSpec((None, PAGE_SIZE, dk), lambda b, c, pt: (pt[b, c * MLA_NP + k], 0, 0))

    seq3 = lambda b, c, pt: (b, 0, 0)
    grid_spec = pltpu.PrefetchScalarGridSpec(
        num_scalar_prefetch=1,
        grid=(DB, n_pages // MLA_NP),
        in_specs=[page_spec(k) for k in range(MLA_NP)] + [
            pl.BlockSpec((None, rows, KV_LORA), seq3),
            pl.BlockSpec((None, rows, ROPE_A), seq3),
            pl.BlockSpec((None, T, dk), seq3),
            pl.BlockSpec((H_A, KV_LORA, V_A), lambda b, c, pt: (0, 0, 0)),
        ],
        out_specs=pl.BlockSpec((None, T, GROUP_W), seq3),
        scratch_shapes=[pltpu.VMEM((rows, 1), jnp.float32), pltpu.VMEM((rows, 1), jnp.float32),
                        pltpu.VMEM((rows, KV_LORA), jnp.float32)],
    )
    return pl.pallas_call(
        functools.partial(_mla_sample_kernel, t_new=T),
        grid_spec=grid_spec,
        out_shape=jax.ShapeDtypeStruct((DB, T, GROUP_W), jnp.float32),
        compiler_params=pltpu.CompilerParams(
            dimension_semantics=("parallel", "arbitrary"), vmem_limit_bytes=VMEM_LIMIT),
        name="mla_sample",
    )(page_ids, *([pages] * MLA_NP), qa, qp, rows_new, wuv)


MLA_TQ = 256
MLA_TK = 512


def _mla_prompt_kernel(qa_ref, qp_ref, k_ref, wuv_ref, o_ref, m_ref, l_ref, acc_ref):
    f32, bf = jnp.float32, jnp.bfloat16
    rows = H_A * MLA_TQ
    s0 = pl.program_id(1) * MLA_TQ
    qa = qa_ref[...].reshape(rows, KV_LORA).astype(bf)
    qp = qp_ref[...].reshape(rows, ROPE_A).astype(bf)
    qpos = s0 + lax.broadcasted_iota(jnp.int32, (MLA_TQ, 1), 0)
    m_ref[...] = jnp.full(m_ref.shape, NEG, f32)
    l_ref[...] = jnp.zeros(l_ref.shape, f32)
    acc_ref[...] = jnp.zeros(acc_ref.shape, f32)

    def body(k, carry):
        k0 = pl.multiple_of(k * MLA_TK, MLA_TK)
        kc = k_ref[pl.ds(k0, MLA_TK), 0:KV_LORA].astype(bf)
        kp = k_ref[pl.ds(k0, MLA_TK), KV_LORA:KV_LORA + ROPE_A].astype(bf)
        s = (lax.dot_general(qa, kc, _NT, preferred_element_type=f32)
             + lax.dot_general(qp, kp, _NT, preferred_element_type=f32)) * MLA_SCALE
        kpos = k0 + lax.broadcasted_iota(jnp.int32, (1, MLA_TK), 1)
        mask = (kpos <= qpos)[None]
        s3 = jnp.where(mask, s.reshape(H_A, MLA_TQ, MLA_TK), NEG)
        m_old = m_ref[...].reshape(H_A, MLA_TQ, 1)
        m_new = jnp.maximum(m_old, jnp.max(s3, axis=-1, keepdims=True))
        alpha = jnp.exp(m_old - m_new)
        p = jnp.where(mask, jnp.exp(s3 - m_new), 0.0)
        l_old = l_ref[...].reshape(H_A, MLA_TQ, 1)
        l_ref[...] = (alpha * l_old + jnp.sum(p, axis=-1, keepdims=True)).reshape(rows, 1)
        pv = jnp.dot(p.reshape(rows, MLA_TK).astype(bf), kc, preferred_element_type=f32)
        acc_ref[...] = alpha.reshape(rows, 1) * acc_ref[...] + pv
        m_ref[...] = m_new.reshape(rows, 1)
        return carry

    lax.fori_loop(0, (s0 + MLA_TQ + MLA_TK - 1) // MLA_TK, body, 0)
    o = (acc_ref[...] * (1.0 / jnp.maximum(l_ref[...], 1e-30))).astype(bf)
    for h in range(H_A):
        o_ref[:, h * V_A:(h + 1) * V_A] = jnp.dot(o[h * MLA_TQ:(h + 1) * MLA_TQ], wuv_ref[h].astype(bf),
                                                  preferred_element_type=f32)


def mla_prompt_attend(q_abs, q_pe, rows_kv, w_uv):
    B, S = q_abs.shape[0], q_abs.shape[1]
    assert S % MLA_TK == 0 and MLA_TK % MLA_TQ == 0
    dk = KV_LORA + ROPE_A
    rows = H_A * MLA_TQ
    qa = q_abs.transpose(0, 2, 1, 3)
    qp = q_pe.transpose(0, 2, 1, 3)
    wuv = w_uv.transpose(1, 0, 2)
    return pl.pallas_call(
        _mla_prompt_kernel,
        grid=(B, S // MLA_TQ),
        in_specs=[
            pl.BlockSpec((None, H_A, MLA_TQ, KV_LORA), lambda b, i: (b, 0, i, 0)),
            pl.BlockSpec((None, H_A, MLA_TQ, ROPE_A), lambda b, i: (b, 0, i, 0)),
            pl.BlockSpec((None, S, dk), lambda b, i: (b, 0, 0)),
            pl.BlockSpec((H_A, KV_LORA, V_A), lambda b, i: (0, 0, 0)),
        ],
        out_specs=pl.BlockSpec((None, MLA_TQ, GROUP_W), lambda b, i: (b, i, 0)),
        out_shape=jax.ShapeDtypeStruct((B, S, GROUP_W), jnp.float32),
        scratch_shapes=[pltpu.VMEM((rows, 1), jnp.float32), pltpu.VMEM((rows, 1), jnp.float32),
                        pltpu.VMEM((rows, KV_LORA), jnp.float32)],
        compiler_params=pltpu.CompilerParams(
            dimension_semantics=("parallel", "arbitrary"), vmem_limit_bytes=VMEM_LIMIT),
        name="mla_prompt",
    )(qa, qp, rows_kv, wuv)


def rmsnorm(x, g):
    xf = x.astype(jnp.float32)
    y = xf * lax.rsqrt(jnp.mean(xf * xf, axis=-1, keepdims=True) + EPS)
    return (y * g.astype(jnp.float32)).astype(x.dtype)


def rope(x, pos, rot):
    half = rot // 2
    inv = ROPE_THETA ** (-2.0 * jnp.arange(half, dtype=jnp.float32) / rot)
    ang = pos.astype(jnp.float32)[:, None] * inv[None, :]
    shape = (pos.shape[0],) + (1,) * (x.ndim - 3) + (half,)
    cos, sin = jnp.cos(ang).reshape(shape), jnp.sin(ang).reshape(shape)
    xf = x.astype(jnp.float32)
    x1, x2 = xf[..., :half], xf[..., half:rot]
    return jnp.concatenate([x1 * cos - x2 * sin, x2 * cos + x1 * sin, xf[..., rot:]], axis=-1).astype(x.dtype)


def masked_softmax(s, mask):
    s = jnp.where(mask, s.astype(jnp.float32), -jnp.inf)
    m = jnp.max(s, axis=-1, keepdims=True)
    e = jnp.exp(s - jnp.where(jnp.isfinite(m), m, 0.0))
    return e / jnp.maximum(jnp.sum(e, axis=-1, keepdims=True), 1e-30)


def project(x, pos, norm_g, w_in_p, q_norm, kv_norm, w_uq, w_uk, gate_bias):
    B, T = x.shape[0], x.shape[1]
    z = dense(x.reshape(B * T, D_MODEL), w_in_p, norm_g=norm_g, tn_cap=640).reshape(B, T, P_IN_PAD)
    idx, acc = [], 0
    for n in IN_SIZES:
        acc += n
        idx.append(acc)
    c_q, c_kv, k_pe, q_b, kv_b, g_b, x_c, x_d, _ = jnp.split(z, idx, axis=-1)
    qa = jnp.einsum('btr,rhd->bthd', rmsnorm(c_q, q_norm), w_uq)
    q_abs = jnp.einsum('bthd,rhd->bthr', qa[..., :NOPE_A], w_uk)
    q_pe = rope(qa[..., NOPE_A:], pos, ROPE_A)
    ckv = rmsnorm(c_kv, kv_norm)
    kpe = rope(k_pe, pos, ROPE_A)
    qn = rope(q_b.reshape(B, T, H_B, HD_B), pos, ROT_B)
    kv = kv_b.reshape(B, T, 3, 2, HD_B)
    kv = jnp.stack([rope(kv[:, :, :, 0], pos, ROT_B), kv[:, :, :, 1]], axis=3).reshape(B, T, 6, HD_B)
    gates = jax.nn.sigmoid(g_b + gate_bias).reshape(B, T, H_B, 3)
    return q_abs, q_pe, ckv, kpe, qn, kv, gates, x_c, x_d


def mla_attend(q_abs, q_pe, ckv, kpe, mask):
    s = (jnp.einsum('...thr,...kr->...htk', q_abs, ckv)
         + jnp.einsum('...thp,...kp->...htk', q_pe, kpe)) * MLA_SCALE
    p = masked_softmax(s, mask)
    return jnp.einsum('...htk,...kr->...thr', p.astype(ckv.dtype), ckv)


def mla_prompt(q_abs, q_pe, ckv, kpe):
    B, S = q_abs.shape[0], q_abs.shape[1]
    nb = S // Q_BLOCK
    qb = q_abs.reshape(B, nb, Q_BLOCK, H_A, KV_LORA).swapaxes(0, 1)
    pb = q_pe.reshape(B, nb, Q_BLOCK, H_A, ROPE_A).swapaxes(0, 1)
    k_pos = jnp.arange(S)

    def block(args):
        qa, qp, i = args
        q_pos = i * Q_BLOCK + jnp.arange(Q_BLOCK)
        return mla_attend(qa, qp, ckv, kpe, k_pos[None, :] <= q_pos[:, None])

    o = lax.map(block, (qb, pb, jnp.arange(nb)))
    return o.swapaxes(0, 1).reshape(B, S, H_A, KV_LORA)


def mla_sample(q_abs, q_pe, ckv_new, kpe_new, cache, layer, page_table):
    T = q_abs.shape[1]
    q_pos = PAST_LEN + jnp.arange(T)
    k_pos = jnp.arange(PAST_LEN + T)
    mask = k_pos[None, :] <= q_pos[:, None]

    def one(args):
        qa, qp, cn, kn, pt = args
        rows = cache[layer, pt].reshape(-1, KV_LORA + ROPE_A)
        ckv = jnp.concatenate([rows[:, :KV_LORA].astype(cn.dtype), cn], axis=0)
        kpe = jnp.concatenate([rows[:, KV_LORA:].astype(kn.dtype), kn], axis=0)
        return mla_attend(qa, qp, ckv, kpe, mask)

    return lax.map(one, (q_abs, q_pe, ckv_new, kpe_new, page_table))


def mla_out(o_lat, w_uv):
    B, T = o_lat.shape[0], o_lat.shape[1]
    return jnp.einsum('bthr,rhd->bthd', o_lat, w_uv).reshape(B, T, GROUP_W)


def nsa_compress(k, pe, w1, b1, w2):
    n_cmp = (k.shape[0] - L_CMP) // D_CMP + 1
    idx = (jnp.arange(n_cmp) * D_CMP)[:, None] + jnp.arange(L_CMP)[None, :]
    hid = jax.nn.gelu((k[idx] + pe) @ w1 + b1)
    return hid.reshape(n_cmp, L_CMP * HD_B) @ w2


def compress_kv(k, v, pe, w1, b1, w2):
    return (nsa_compress(k, pe[0], w1[0], b1[0], w2[0]), nsa_compress(v, pe[1], w1[1], b1[1], w2[1]))


def nsa_cmp_sel(q, q_pos, kc, vc, k_sel, v_sel):
    T = q.shape[0]
    n_cmp, n = kc.shape[0], k_sel.shape[0]
    n_blk = -(-n // L_SEL)
    c_end = jnp.arange(n_cmp) * D_CMP + (L_CMP - 1)
    p_c = masked_softmax(jnp.einsum('thd,nd->htn', q, kc) * NSA_SCALE, c_end[None, :] <= q_pos[:, None])
    o_cmp = jnp.einsum('htn,nd->thd', p_c.astype(vc.dtype), vc)
    ci = jnp.arange(n_cmp)[:, None] * D_CMP
    bj = jnp.arange(n_blk)[None, :] * L_SEL
    cover = ((ci < bj + L_SEL) & (ci + L_CMP > bj)).astype(jnp.float32)
    imp = jnp.sum(p_c, axis=0) @ cover
    cur = (q_pos // L_SEL)[:, None]
    j = jnp.arange(n_blk)[None, :]
    causal = j <= cur
    forced = (j == 0) | (j == cur) | (j == cur - 1)
    score = jnp.where(forced, FORCE, jnp.where(causal, imp, -FORCE))
    k_top = min(N_SEL, n_blk)
    _, sel = lax.top_k(score, k_top)
    sel_ok = jnp.take_along_axis(causal, sel, axis=1)
    pos = (sel[:, :, None] * L_SEL + jnp.arange(L_SEL)[None, None, :]).reshape(T, k_top * L_SEL)
    ok = jnp.repeat(sel_ok, L_SEL, axis=1) & (pos <= q_pos[:, None])
    pos_c = jnp.minimum(pos, n - 1)
    kg, vg = k_sel[pos_c], v_sel[pos_c]
    p_s = masked_softmax(jnp.einsum('thd,tkd->htk', q, kg) * NSA_SCALE, ok[None])
    o_sel = jnp.einsum('htk,tkd->thd', p_s.astype(vg.dtype), vg)
    return o_cmp, o_sel


def window_attend(q, q_pos, k, v, k_pos):
    ok = ((k_pos[None, :] <= q_pos[:, None]) & (k_pos[None, :] > q_pos[:, None] - WINDOW)
          & (k_pos[None, :] >= 0))
    p = masked_softmax(jnp.einsum('bthd,bkd->bhtk', q, k) * NSA_SCALE, ok)
    return jnp.einsum('bhtk,bkd->bthd', p.astype(v.dtype), v)


def nsa_prompt(q, kv, pe, w1, b1, w2):
    B, S = q.shape[0], q.shape[1]
    kc, vc = jax.vmap(compress_kv, in_axes=(0, 0, None, None, None, None))(kv[:, :, 0], kv[:, :, 1], pe, w1, b1, w2)
    k_sel, v_sel = kv[:, :, 2], kv[:, :, 3]
    pad = ((0, 0), (WINDOW, 0), (0, 0))
    k_win, v_win = jnp.pad(kv[:, :, 4], pad), jnp.pad(kv[:, :, 5], pad)
    nb = S // Q_BLOCK
    qb = q.reshape(B, nb, Q_BLOCK, H_B, HD_B).swapaxes(0, 1)
    sparse = jax.vmap(nsa_cmp_sel, in_axes=(0, None, 0, 0, 0, 0))

    def block(args):
        qq, i = args
        start = i * Q_BLOCK
        q_pos = start + jnp.arange(Q_BLOCK)
        o_c, o_s = sparse(qq, q_pos, kc, vc, k_sel, v_sel)
        k_pos = start - WINDOW + jnp.arange(WINDOW + Q_BLOCK)
        kb = lax.dynamic_slice_in_dim(k_win, start, WINDOW + Q_BLOCK, axis=1)
        vb = lax.dynamic_slice_in_dim(v_win, start, WINDOW + Q_BLOCK, axis=1)
        return o_c, o_s, window_attend(qq, q_pos, kb, vb, k_pos)

    o_c, o_s, o_w = lax.map(block, (qb, jnp.arange(nb)))
    unblock = lambda o: o.swapaxes(0, 1).reshape(B, S, H_B, HD_B)
    return unblock(o_c), unblock(o_s), unblock(o_w)


def nsa_sample(q, kv, cache, layer, page_table, win_buf, pe, w1, b1, w2):
    T = q.shape[1]
    q_pos = PAST_LEN + jnp.arange(T)

    def one(args):
        qq, kn, pt = args
        rows = cache[layer, pt].reshape(-1, 4, HD_B).astype(kn.dtype)
        full = jnp.concatenate([rows, kn[:, :4]], axis=0)
        kc, vc = compress_kv(full[:, 0], full[:, 1], pe, w1, b1, w2)
        return nsa_cmp_sel(qq, q_pos, kc, vc, full[:, 2], full[:, 3])

    o_c, o_s = lax.map(one, (q, kv, page_table))
    lb = win_buf.shape[1]
    wkv = jnp.concatenate([win_buf.astype(kv.dtype), kv[:, :, 4:6]], axis=1)
    k_pos = PAST_LEN - lb + jnp.arange(lb + T)
    o_w = window_attend(q, q_pos, wkv[:, :, 0], wkv[:, :, 1], k_pos)
    return o_c, o_s, o_w, wkv[:, -lb:]


def nsa_merge(gates, o_c, o_s, o_w):
    B, T = o_c.shape[0], o_c.shape[1]
    o = gates[..., 0:1] * o_c + gates[..., 1:2] * o_s + gates[..., 2:3] * o_w
    return o.reshape(B, T, GROUP_W)


def pool_mix(xs, w, scale):
    B, L = xs.shape[0], xs.shape[1]
    xf = xs.astype(jnp.float32)
    cs0 = jnp.pad(jnp.cumsum(xf, axis=1), ((0, 0), (1, 0), (0, 0)))
    r = jnp.arange(L)
    parts = []
    for g, win in enumerate(POOL_WINDOWS):
        c0, c1 = g * POOL_GW, (g + 1) * POOL_GW
        start = jnp.maximum(r + 1 - win, 0)
        cnt = (r + 1 - start).astype(jnp.float32)[None, :, None]
        mean = (cs0[:, 1:, c0:c1] - cs0[:, start, c0:c1]) / cnt
        parts.append(mean - xf[:, :, c0:c1])
    pooled = jnp.stack(parts, axis=2).astype(xs.dtype)
    y = jnp.einsum('blgc,gcd->blgd', pooled, w).reshape(B, L, GROUP_W)
    return y * scale


def gmlp_mix(zd, v_norm, ws, bs):
    B, L = zd.shape[0], zd.shape[1]
    u, v = jnp.split(jax.nn.gelu(zd), 2, axis=-1)
    v = rmsnorm(v, v_norm)
    n = min(L, CHUNK)
    wm = ws[:, :n, :n] * jnp.tril(jnp.ones((n, n), ws.dtype))
    vc = v.reshape(B, L // n, n, G_D, GD_W)
    s = jnp.einsum('gts,bcsgd->bctgd', wm, vc) + bs[:, :n].T[None, None, :, :, None]
    return u * s.reshape(B, L, GROUP_W), v


def mem_attend(x, norm_g, mkv, wq, wo, layer):
    B, T = x.shape[0], x.shape[1]
    x2 = x.reshape(B * T, D_MODEL)
    q = dense(x2, wq, layer, norm_g=norm_g).reshape(B, T, H_M, HD_M)
    k = mkv[:, :, 0].reshape(B, -1, H_M, HD_M)
    v = mkv[:, :, 1].reshape(B, -1, H_M, HD_M)
    p = jax.nn.softmax(jnp.einsum('bthd,bmhd->bhtm', q, k).astype(jnp.float32) * MEM_SCALE, axis=-1)
    o = jnp.einsum('bhtm,bmhd->bthd', p.astype(v.dtype), v).reshape(B * T, H_M * HD_M)
    return dense(o, wo, layer, residual=x2).reshape(B, T, D_MODEL)


def conv_ffn(x, norm_g, buf, w_up, conv_w, conv_b, w_down, layer):
    B, L = x.shape[0], x.shape[1]
    x2 = x.reshape(B * L, D_MODEL)
    a = dense(x2, w_up, layer, norm_g=norm_g).reshape(B, L, 2 * D_FF)
    ap = jnp.concatenate([buf.astype(a.dtype), a], axis=1)
    c = conv_b
    for k in range(CONV_W):
        c = c + ap[:, k:k + L] * conv_w[k]
    g, u = jnp.split(c, 2, axis=-1)
    act = (jax.nn.silu(g) * u).reshape(B * L, D_FF).astype(jnp.bfloat16)
    y = dense(act, w_down, layer, residual=x2, tm=512)
    return y.reshape(B, L, D_MODEL), ap[:, L:]


def kernel(x_prompt, x_sample, cache_mla, cache_nsa, state_nsa_win, state_pool, state_conv, cache_mem_kv,
           page_table, mem_prompt, mix_norm, w_in, mla_q_norm, mla_kv_norm, mla_w_uq, mla_w_ukv,
           nsa_gate_bias, nsa_cmp_pe, nsa_cmp_w1, nsa_cmp_b1, nsa_cmp_w2, pool_w, pool_scale,
           gmlp_v_norm, gmlp_ws, gmlp_bs, mix_out_norm, w_out, mem_norm, mem_kv_norm, mem_wq, mem_wkv,
           mem_wo, ffn_norm, ffn_w_up, ffn_conv_w, ffn_conv_b, ffn_w_down, final_norm):
    B, S = x_prompt.shape[0], x_prompt.shape[1]
    DB, T = x_sample.shape[0], x_sample.shape[1]
    pos_p = jnp.arange(S, dtype=jnp.int32)
    pos_s = PAST_LEN + jnp.arange(T, dtype=jnp.int32)
    lb_p = min(WINDOW, S)
    xp, xs = x_prompt, x_sample
    mla_p, mla_s, nsa_p, nsa_s, win_p, win_s = [], [], [], [], [], []
    pool_p, pool_s, gv_s, conv_p, conv_s, memkv_p = [], [], [], [], [], []
    for l in range(DEPTH):
        w_uk = mla_w_ukv[l][..., :NOPE_A]
        w_uv = mla_w_ukv[l][..., NOPE_A:]
        cmp = (nsa_cmp_pe[l], nsa_cmp_w1[l], nsa_cmp_b1[l], nsa_cmp_w2[l])
        w_in_p = jnp.pad(w_in[l], ((0, 0), (0, P_IN_PAD - P_IN)))

        def mix_merge(x, oa, ob, oc, od):
            Bx, Tx = x.shape[0], x.shape[1]
            o = jnp.concatenate([oa, ob, oc, od], axis=-1).reshape(Bx * Tx, D_MODEL)
            return dense(o, w_out, l, norm_g=mix_out_norm[l], n_groups=4,
                         residual=x.reshape(Bx * Tx, D_MODEL)).reshape(Bx, Tx, D_MODEL)

        qa, qpe, ckv, kpe, qb, kv, gt, xc, xd = project(xp, pos_p, mix_norm[l], w_in_p, mla_q_norm[l],
                                                        mla_kv_norm[l], mla_w_uq[l], w_uk, nsa_gate_bias[l])
        rows_p = jnp.concatenate([ckv, kpe], axis=-1)
        oa = mla_prompt_attend(qa, qpe, rows_p, w_uv)
        kc, vc = jax.vmap(compress_kv, in_axes=(0, 0, None, None, None, None))(kv[:, :, 0], kv[:, :, 1], *cmp)
        ob = nsa_prompt_attend(qb, kv, kc, vc, gt)
        oc = pool_mix(xc, pool_w[l], pool_scale[l])
        od, _ = gmlp_mix(xd, gmlp_v_norm[l], gmlp_ws[l], gmlp_bs[l])
        xp = mix_merge(xp, oa, ob, oc, od)
        mla_p.append(rows_p)
        nsa_p.append(kv[:, :, :4])
        win_p.append(kv[:, S - lb_p:, 4:6])
        pool_p.append(xc[:, S - POOL_BUF:])

        qa, qpe, ckv, kpe, qb, kv, gt, xc, xd = project(xs, pos_s, mix_norm[l], w_in_p, mla_q_norm[l],
                                                        mla_kv_norm[l], mla_w_uq[l], w_uk, nsa_gate_bias[l])
        rows_s = jnp.concatenate([ckv, kpe], axis=-1)
        oa = mla_sample_attend(qa, qpe, rows_s, cache_mla, l, page_table, w_uv)
        ob = nsa_sample_attend(qb, kv, gt, cache_nsa, l, page_table, state_nsa_win[l], *cmp)
        new_win = jnp.concatenate([state_nsa_win[l], kv[:, :, 4:6]], axis=1)[:, T:]
        xcat = jnp.concatenate([state_pool[l].astype(xc.dtype), xc], axis=1)
        oc = pool_mix(xcat, pool_w[l], pool_scale[l])[:, POOL_BUF:]
        od, v_rows = gmlp_mix(xd, gmlp_v_norm[l], gmlp_ws[l], gmlp_bs[l])
        xs = mix_merge(xs, oa, ob, oc, od)
        mla_s.append(rows_s)
        nsa_s.append(kv[:, :, :4])
        win_s.append(new_win)
        pool_s.append(xcat[:, xcat.shape[1] - POOL_BUF:])
        gv_s.append(v_rows)

        mkv = dense(mem_prompt.reshape(B * MEM_LEN, D_MODEL), mem_wkv, l,
                    norm_g=mem_kv_norm[l]).reshape(B, MEM_LEN, 2, H_M * HD_M)
        xp = mem_attend(xp, mem_norm[l], mkv, mem_wq, mem_wo, l)
        xs = mem_attend(xs, mem_norm[l], cache_mem_kv[l].astype(xs.dtype), mem_wq, mem_wo, l)
        memkv_p.append(mkv)

        xp, cbp = conv_ffn(xp, ffn_norm[l], jnp.zeros((B, CONV_W - 1, 2 * D_FF), xp.dtype), ffn_w_up,
                           ffn_conv_w[l], ffn_conv_b[l], ffn_w_down, l)
        xs, cbs = conv_ffn(xs, ffn_norm[l], state_conv[l], ffn_w_up, ffn_conv_w[l], ffn_conv_b[l], ffn_w_down, l)
        conv_p.append(cbp)
        conv_s.append(cbs)

    y_prompt = rmsnorm(xp, final_norm)
    y_sample = rmsnorm(xs, final_norm)
    return (y_prompt, y_sample, jnp.stack(mla_p), jnp.stack(mla_s), jnp.stack(nsa_p), jnp.stack(nsa_s),
            jnp.stack(win_p), jnp.stack(win_s), jnp.stack(pool_p), jnp.stack(pool_s), jnp.stack(gv_s),
            jnp.stack(conv_p), jnp.stack(conv_s), jnp.stack(memkv_p))
```

```python
import functools

import jax
import jax.numpy as jnp
import numpy as np
from jax import lax
from jax.experimental import pallas as pl
from jax.experimental.pallas import tpu as pltpu

D_MODEL = 2048
DEPTH = 2
PAST_LEN = 16384
PAGE_SIZE = 128
GROUP_W = 512
H_A = 4
NOPE_A = 128
ROPE_A = 64
V_A = 128
Q_LORA = 384
KV_LORA = 256
MLA_SCALE = (NOPE_A + ROPE_A) ** -0.5
H_B = 8
HD_B = 64
ROT_B = 16
L_CMP = 32
D_CMP = 16
L_SEL = 64
N_SEL = 16
WINDOW = 512
NSA_SCALE = HD_B ** -0.5
FORCE = 1e9
POOL_WINDOWS = (2, 4, 8, 16)
POOL_GW = 128
POOL_BUF = 15
CHUNK = 128
G_D = 4
GD_W = 128
MEM_LEN = 256
H_M = 4
HD_M = 128
MEM_SCALE = HD_M ** -0.5
D_FF = 5632
CONV_W = 3
ROPE_THETA = 500000.0
EPS = 1e-6
Q_BLOCK = 128
IN_SIZES = (Q_LORA, KV_LORA, ROPE_A, H_B * HD_B, 6 * HD_B, 3 * H_B, GROUP_W, 2 * GROUP_W)
P_IN = sum(IN_SIZES)
P_IN_PAD = 3200

V7X_VMEM_BYTES = 64 * 1024 * 1024
LANE = 128
VMEM_LIMIT = 56 * 1024 * 1024


def _dense_kernel(*refs, n_groups, has_norm, has_res):
    it = iter(refs)
    x_ref = next(it)
    g_ref = next(it) if has_norm else None
    w_ref = next(it)
    r_ref = next(it) if has_res else None
    o_ref = next(it)
    xn_ref = next(it)

    @pl.when(pl.program_id(1) == 0)
    def _():
        x = x_ref[...]
        if has_norm:
            xf = x.astype(jnp.float32)
            gw = xf.shape[-1] // n_groups
            parts = []
            for gi in range(n_groups):
                xg = xf[:, gi * gw:(gi + 1) * gw]
                ms = jnp.mean(xg * xg, axis=-1, keepdims=True)
                parts.append(xg * lax.rsqrt(ms + EPS) * g_ref[:, gi * gw:(gi + 1) * gw])
            xf = parts[0] if n_groups == 1 else jnp.concatenate(parts, axis=-1)
            xn_ref[...] = xf.astype(jnp.bfloat16)
        else:
            xn_ref[...] = x.astype(jnp.bfloat16)

    acc = jnp.dot(xn_ref[...], w_ref[...].astype(jnp.bfloat16), preferred_element_type=jnp.float32)
    if has_res:
        acc = acc + r_ref[...]
    o_ref[...] = acc


def _pick_tile(n, cap):
    best = None
    for t in range(LANE, min(n, cap) + 1, LANE):
        if n % t == 0:
            best = t
    return best if best is not None else n


def dense(x, w, layer=None, norm_g=None, n_groups=1, residual=None, tm=1024, tn_cap=512):
    m, k = x.shape
    n = w.shape[-1]
    tm = min(tm, m)
    assert m % tm == 0
    tn = _pick_tile(n, tn_cap)
    has_norm = norm_g is not None
    has_res = residual is not None
    in_specs = [pl.BlockSpec((tm, k), lambda i, j: (i, 0))]
    args = [x]
    if has_norm:
        in_specs.append(pl.BlockSpec((1, k), lambda i, j: (0, 0)))
        args.append(norm_g.reshape(1, k).astype(jnp.float32))
    if layer is None:
        in_specs.append(pl.BlockSpec((k, tn), lambda i, j: (0, j)))
    else:
        in_specs.append(pl.BlockSpec((None, k, tn), lambda i, j: (layer, 0, j)))
    args.append(w)
    if has_res:
        in_specs.append(pl.BlockSpec((tm, tn), lambda i, j: (i, j)))
        args.append(residual)
    return pl.pallas_call(
        functools.partial(_dense_kernel, n_groups=n_groups, has_norm=has_norm, has_res=has_res),
        grid=(m // tm, n // tn),
        in_specs=in_specs,
        out_specs=pl.BlockSpec((tm, tn), lambda i, j: (i, j)),
        out_shape=jax.ShapeDtypeStruct((m, n), jnp.float32),
        scratch_shapes=[pltpu.VMEM((tm, k), jnp.bfloat16)],
        compiler_params=pltpu.CompilerParams(
            dimension_semantics=("parallel", "arbitrary"), vmem_limit_bytes=VMEM_LIMIT),
        name="dense",
    )(*args)


NSA_TQ = 128
NSA_TK = 512
NEG = -1e30
_NT = (((1,), (1,)), ((), ()))
_NN = (((1,), (0,)), ((), ()))


def _softmax_rows(s3, mask2):
    sm = jnp.where(mask2[None], s3, NEG)
    m = jnp.max(sm, axis=-1, keepdims=True)
    e = jnp.where(mask2[None], jnp.exp(sm - m), 0.0)
    den = jnp.sum(e, axis=-1, keepdims=True)
    return e * (1.0 / jnp.maximum(den, 1e-30))


def _nsa_prompt_kernel(q_ref, sel_ref, win_ref, kcvc_ref, g_ref, e_ref, cover_ref, o_ref,
                       m_ref, l_ref, acc_ref, *, n_cmp, n_blk):
    f32, bf = jnp.float32, jnp.bfloat16
    rows = H_B * NSA_TQ
    s0 = pl.program_id(1) * NSA_TQ
    q = q_ref[...].reshape(rows, 2 * HD_B).astype(bf)
    qpos = s0 + lax.broadcasted_iota(jnp.int32, (NSA_TQ, 1), 0)

    n_pad = kcvc_ref.shape[0]
    kcvc = kcvc_ref[...].astype(bf)
    sc = lax.dot_general(q, kcvc, _NT, preferred_element_type=f32) * NSA_SCALE
    n_idx = lax.broadcasted_iota(jnp.int32, (NSA_TQ, n_pad), 1)
    cmask = (n_idx * D_CMP + (L_CMP - 1) <= qpos) & (n_idx < n_cmp)
    p_c = _softmax_rows(sc.reshape(H_B, NSA_TQ, n_pad), cmask)
    o_c = jnp.dot(p_c.reshape(rows, n_pad).astype(bf), kcvc, preferred_element_type=f32)

    psum = jnp.sum(p_c, axis=0)
    p_hi = psum.astype(bf)
    p_lo = (psum - p_hi.astype(f32)).astype(bf)
    cover = cover_ref[...]
    imp = (jnp.dot(p_hi, cover, preferred_element_type=f32)
           + jnp.dot(p_lo, cover, preferred_element_type=f32))
    j = lax.broadcasted_iota(jnp.int32, (NSA_TQ, n_blk), 1)
    cur = qpos // L_SEL
    causal = j <= cur
    forced = (j == 0) | (j == cur) | (j == cur - 1)
    score = jnp.where(forced, FORCE, jnp.where(causal, imp, -FORCE))
    rank = jnp.zeros((NSA_TQ, n_blk), f32)
    for i in range(n_blk):
        col = score[:, i:i + 1]
        beats = (col > score) | ((col == score) & (j > i))
        rank = rank + jnp.where(beats, 1.0, 0.0)
    selb = jnp.where((rank < float(min(N_SEL, n_blk))) & causal, 1.0, 0.0).astype(bf)

    m_ref[...] = jnp.full(m_ref.shape, NEG, f32)
    l_ref[...] = jnp.zeros(l_ref.shape, f32)
    acc_ref[...] = jnp.zeros(acc_ref.shape, f32)

    def body(k, carry):
        kv = sel_ref[pl.ds(pl.multiple_of(k * NSA_TK, NSA_TK), NSA_TK), :].astype(bf)
        s = lax.dot_general(q, kv, _NT, preferred_element_type=f32) * NSA_SCALE
        em = jnp.dot(selb, e_ref[k], preferred_element_type=f32)
        kpos = k * NSA_TK + lax.broadcasted_iota(jnp.int32, (1, NSA_TK), 1)
        mask = ((em > 0.5) & (kpos <= qpos))[None]
        s3 = jnp.where(mask, s.reshape(H_B, NSA_TQ, NSA_TK), NEG)
        m_old = m_ref[...].reshape(H_B, NSA_TQ, 1)
        m_new = jnp.maximum(m_old, jnp.max(s3, axis=-1, keepdims=True))
        alpha = jnp.exp(m_old - m_new)
        p = jnp.where(mask, jnp.exp(s3 - m_new), 0.0)
        l_old = l_ref[...].reshape(H_B, NSA_TQ, 1)
        l_ref[...] = (alpha * l_old + jnp.sum(p, axis=-1, keepdims=True)).reshape(rows, 1)
        pv = jnp.dot(p.reshape(rows, NSA_TK).astype(bf), kv, preferred_element_type=f32)
        acc_ref[...] = alpha.reshape(rows, 1) * acc_ref[...] + pv
        m_ref[...] = m_new.reshape(rows, 1)
        return carry

    lax.fori_loop(0, (s0 + NSA_TQ + NSA_TK - 1) // NSA_TK, body, 0)
    o_s = acc_ref[...] * (1.0 / jnp.maximum(l_ref[...], 1e-30))

    w0 = pl.multiple_of(jnp.maximum(s0 - WINDOW, 0), NSA_TQ)
    kvw = win_ref[pl.ds(w0, WINDOW + NSA_TQ), :].astype(bf)
    sw = lax.dot_general(q, kvw, _NT, preferred_element_type=f32) * NSA_SCALE
    kpos = w0 + lax.broadcasted_iota(jnp.int32, (1, WINDOW + NSA_TQ), 1)
    wmask = (kpos <= qpos) & (kpos > qpos - WINDOW)
    p_w = _softmax_rows(sw.reshape(H_B, NSA_TQ, WINDOW + NSA_TQ), wmask)
    o_w = jnp.dot(p_w.reshape(rows, WINDOW + NSA_TQ).astype(bf), kvw, preferred_element_type=f32)

    g = g_ref[...].reshape(rows, 3)
    o = (g[:, 0:1] * o_c + g[:, 1:2] * o_s + g[:, 2:3] * o_w).reshape(H_B, NSA_TQ, 2 * HD_B)
    lane = lax.broadcasted_iota(jnp.int32, (NSA_TQ, 2 * HD_B), 1)
    for hp in range(H_B // 2):
        lo = pltpu.roll(o[2 * hp], HD_B, 1)
        o_ref[:, hp * 2 * HD_B:(hp + 1) * 2 * HD_B] = jnp.where(lane < HD_B, lo, o[2 * hp + 1])


def nsa_prompt_attend(qn, kv, kcvc, n_cmp, gates):
    B, S = qn.shape[0], qn.shape[1]
    assert S % NSA_TK == 0 and S >= WINDOW + NSA_TQ
    n_pad = kcvc.shape[1]
    assert n_pad % LANE == 0 and n_cmp <= n_pad
    n_blk = S // L_SEL
    q_pad = jnp.pad(qn.transpose(0, 2, 1, 3), ((0, 0), (0, 0), (0, 0), (0, HD_B)))
    kv3 = kv.reshape(B, S, 6 * HD_B)
    g_t = gates.transpose(0, 2, 1, 3)
    blk_of_key = np.arange(S) // L_SEL
    expand = (np.arange(n_blk)[:, None] == blk_of_key[None, :]).astype(np.float32)
    e3 = jnp.asarray(expand.reshape(n_blk, S // NSA_TK, NSA_TK).transpose(1, 0, 2), jnp.bfloat16)
    ci = np.arange(n_pad)[:, None] * D_CMP
    bj = np.arange(n_blk)[None, :] * L_SEL
    cover = jnp.asarray(((ci < bj + L_SEL) & (ci + L_CMP > bj)).astype(np.float32), jnp.bfloat16)
    rows = H_B * NSA_TQ
    return pl.pallas_call(
        functools.partial(_nsa_prompt_kernel, n_cmp=n_cmp, n_blk=n_blk),
        grid=(B, S // NSA_TQ),
        in_specs=[
            pl.BlockSpec((None, H_B, NSA_TQ, 2 * HD_B), lambda b, i: (b, 0, i, 0)),
            pl.BlockSpec((None, S, 2 * HD_B), lambda b, i: (b, 0, 1)),
            pl.BlockSpec((None, S, 2 * HD_B), lambda b, i: (b, 0, 2)),
            pl.BlockSpec((None, n_pad, 2 * HD_B), lambda b, i: (b, 0, 0)),
            pl.BlockSpec((None, H_B, NSA_TQ, 3), lambda b, i: (b, 0, i, 0)),
            pl.BlockSpec((S // NSA_TK, n_blk, NSA_TK), lambda b, i: (0, 0, 0)),
            pl.BlockSpec((n_pad, n_blk), lambda b, i: (0, 0)),
        ],
        out_specs=pl.BlockSpec((None, NSA_TQ, GROUP_W), lambda b, i: (b, i, 0)),
        out_shape=jax.ShapeDtypeStruct((B, S, GROUP_W), jnp.float32),
        scratch_shapes=[pltpu.VMEM((rows, 1), jnp.float32), pltpu.VMEM((rows, 1), jnp.float32),
                        pltpu.VMEM((rows, 2 * HD_B), jnp.float32)],
        compiler_params=pltpu.CompilerParams(
            dimension_semantics=("parallel", "arbitrary"), vmem_limit_bytes=VMEM_LIMIT),
        name="nsa_prompt",
    )(q_pad, kv3, kv3, kcvc, g_t, e3, cover)


NSA_NP = 64
NSA_TKS = 2048
GROUP_ROWS = D_CMP
assert L_CMP == 2 * D_CMP and PAGE_SIZE % GROUP_ROWS == 0


def _nsa_sample_kernel(pt_ref, *refs, n_pages, t_new):
    del pt_ref
    f32, bf = jnp.float32, jnp.bfloat16
    page_refs = refs[:NSA_NP]
    (q_ref, new_ref, win_ref, g_ref, pe_ref, bd1_ref, b1_ref, w2p_ref, cover_ref, eloc_ref,
     o_ref, ab_ref, ksvs_ref, xrow_ref) = refs[NSA_NP:]
    c = pl.program_id(1)
    gpp = PAGE_SIZE // GROUP_ROWS
    gpc = NSA_NP * gpp
    n_grp = n_pages * gpp
    n_past = n_pages * PAGE_SIZE
    n_blk = n_past // L_SEL
    rows = H_B * t_new

    for k in range(NSA_NP):
        row0 = pl.multiple_of((c * NSA_NP + k) * PAGE_SIZE, PAGE_SIZE)
        ksvs_ref[:, pl.ds(row0, PAGE_SIZE)] = page_refs[k][2 * HD_B:4 * HD_B, :].astype(bf)
        xrow_ref[k * PAGE_SIZE:(k + 1) * PAGE_SIZE, :] = page_refs[k][0:2 * HD_B, :].T

    ab = jnp.zeros((gpc, 4 * HD_B), f32)
    for p in range(GROUP_ROWS):
        xp = xrow_ref[pl.ds(p, gpc, stride=GROUP_ROWS), :]
        x2 =jnp.concatenate([xp, xp], axis=1) + pe_ref[p:p + 1, :]
        h = jax.nn.gelu(jnp.dot(x2.astype(bf), bd1_ref[...], preferred_element_type=f32) + b1_ref[...])
        ab = ab + jnp.dot(h.astype(bf), w2p_ref[p], preferred_element_type=f32)
    ab_ref[pl.ds(pl.multiple_of(c * gpc, gpc), gpc), :] = ab

    @pl.when(c == pl.num_programs(1) - 1)
    def _():
        q = q_ref[...].astype(bf)
        tq = lax.broadcasted_iota(jnp.int32, (t_new, 1), 0)

        abv = ab_ref[...]
        kcvc = (abv[:, 0:2 * HD_B] + pltpu.roll(abv[:, 2 * HD_B:4 * HD_B], n_grp - 1, 0)).astype(bf)
        sc = lax.dot_general(q, kcvc, _NT, preferred_element_type=f32) * NSA_SCALE
        cmask = lax.broadcasted_iota(jnp.int32, (t_new, n_grp), 1) < n_grp - 1
        p_c = _softmax_rows(sc.reshape(H_B, t_new, n_grp), cmask)
        o_c = jnp.dot(p_c.reshape(rows, n_grp).astype(bf), kcvc, preferred_element_type=f32)

        psum = jnp.sum(p_c, axis=0)
        p_hi = psum.astype(bf)
        p_lo = (psum - p_hi.astype(f32)).astype(bf)
        cover = cover_ref[...]
        imp = (jnp.dot(p_hi, cover, preferred_element_type=f32)
               + jnp.dot(p_lo, cover, preferred_element_type=f32))
        j = lax.broadcasted_iota(jnp.int32, (t_new, n_blk), 1)
        score = jnp.where((j == 0) | (j == n_blk - 1), FORCE, imp)
        rank = jnp.zeros((t_new, n_blk), f32)
        for i in range(n_blk):
            col = score[:, i:i + 1]
            beats = (col > score) | ((col == score) & (j > i))
            rank = rank + jnp.where(beats, 1.0, 0.0)
        sel = jnp.where(rank < float(N_SEL - 1), 1.0, 0.0)

        m = jnp.full((H_B, t_new, 1), NEG, f32)
        l = jnp.zeros((H_B, t_new, 1), f32)
        acc = jnp.zeros((rows, 2 * HD_B), f32)
        bpt = NSA_TKS // L_SEL

        def fold(m, l, acc, s, mask, kv, kv_dims):
            s3 = jnp.where(mask, s.reshape(H_B, t_new, s.shape[-1]), NEG)
            m_new = jnp.maximum(m, jnp.max(s3, axis=-1, keepdims=True))
            alpha = jnp.exp(m - m_new)
            p = jnp.where(mask, jnp.exp(s3 - m_new), 0.0)
            l = alpha * l + jnp.sum(p, axis=-1, keepdims=True)
            pv = lax.dot_general(p.reshape(rows, s.shape[-1]).astype(bf), kv, kv_dims, preferred_element_type=f32)
            return m_new, l, alpha.reshape(rows, 1) * acc + pv

        for k in range(n_past // NSA_TKS):
            kvt = ksvs_ref[:, k * NSA_TKS:(k + 1) * NSA_TKS]
            s = jnp.dot(q, kvt, preferred_element_type=f32) * NSA_SCALE
            em = jnp.dot(sel[:, k * bpt:(k + 1) * bpt].astype(bf), eloc_ref[...], preferred_element_type=f32)
            m, l, acc = fold(m, l, acc, s, (em > 0.5)[None], kvt, _NT)

        new = new_ref[...]
        pad = jnp.zeros((LANE - t_new, 2 * HD_B), f32)
        kv = jnp.concatenate([new[:, 2 * HD_B:4 * HD_B], pad], axis=0).astype(bf)
        s = lax.dot_general(q, kv, _NT, preferred_element_type=f32) * NSA_SCALE
        kidx = lax.broadcasted_iota(jnp.int32, (1, LANE), 1)
        m, l, acc = fold(m, l, acc, s, (kidx <= tq)[None], kv, _NN)
        o_s = acc * (1.0 / jnp.maximum(l, 1e-30)).reshape(rows, 1)

        lb = win_ref.shape[0]
        kvw = jnp.concatenate([win_ref[...], new[:, 4 * HD_B:6 * HD_B], pad], axis=0).astype(bf)
        sw = lax.dot_general(q, kvw, _NT, preferred_element_type=f32) * NSA_SCALE
        kidx = lax.broadcasted_iota(jnp.int32, (1, lb + LANE), 1)
        wmask = (kidx <= lb + tq) & (kidx > lb + tq - WINDOW)
        p_w = _softmax_rows(sw.reshape(H_B, t_new, lb + LANE), wmask)
        o_w = jnp.dot(p_w.reshape(rows, lb + LANE).astype(bf), kvw, preferred_element_type=f32)

        g = g_ref[...]
        o = (g[:, 0:1] * o_c + g[:, 1:2] * o_s + g[:, 2:3] * o_w).reshape(H_B, t_new, 2 * HD_B)
        lane = lax.broadcasted_iota(jnp.int32, (t_new, 2 * HD_B), 1)
        for hp in range(H_B // 2):
            lo = pltpu.roll(o[2 * hp], HD_B, 1)
            o_ref[:, hp * 2 * HD_B:(hp + 1) * 2 * HD_B] = jnp.where(lane < HD_B, lo, o[2 * hp + 1])


def _blockdiag(blocks):
    n = len(blocks)
    r, c = blocks[0].shape
    out = jnp.zeros((n * r, n * c), blocks[0].dtype)
    for i, b in enumerate(blocks):
        out = out.at[i * r:(i + 1) * r, i * c:(i + 1) * c].set(b)
    return out


def nsa_sample_attend(qn, kv_new, gates, cache, layer, page_table, win_buf, pe, w1, b1, w2):
    DB, T = qn.shape[0], qn.shape[1]
    n_pages = page_table.shape[1]
    n_past = n_pages * PAGE_SIZE
    lb = win_buf.shape[1]
    assert n_pages % NSA_NP == 0 and n_past % NSA_TKS == 0 and n_past % L_SEL == 0 and T <= L_SEL
    assert T % 8 == 0 and lb == WINDOW and n_past // L_SEL + 1 >= N_SEL
    assert (n_past + T - L_CMP) // D_CMP + 1 == n_past // D_CMP - 1
    n_grp = n_past // GROUP_ROWS
    n_blk = n_past // L_SEL
    bf = jnp.bfloat16
    rows = H_B * T
    q_pad = jnp.pad(qn.transpose(0, 2, 1, 3).reshape(DB, rows, HD_B), ((0, 0), (0, 0), (0, HD_B)))
    g_t = gates.transpose(0, 2, 1, 3).reshape(DB, rows, 3)
    new3 = kv_new.reshape(DB, T, 6 * HD_B)
    win3 = win_buf.reshape(DB, lb, 2 * HD_B)
    pages = cache.transpose(0, 1, 3, 4, 2).reshape(cache.shape[0] * cache.shape[1], 4 * HD_B, PAGE_SIZE)
    page_ids = page_table + layer * cache.shape[1]
    pe2, bd1, b1x, w2p = _compress_weights(pe, w1, b1, w2)
    ci = np.arange(n_grp)[:, None] * D_CMP
    bj = np.arange(n_blk)[None, :] * L_SEL
    cover = jnp.asarray(((ci < bj + L_SEL) & (ci + L_CMP > bj) & (ci < (n_grp - 1) * D_CMP)).astype(np.float32), bf)
    eloc = jnp.asarray((np.arange(NSA_TKS // L_SEL)[:, None] == (np.arange(NSA_TKS) // L_SEL)[None, :])
                       .astype(np.float32), bf)

    def page_spec(k):
        return pl.BlockSpec((None, 4 * HD_B, PAGE_SIZE), lambda b, c, pt: (pt[b, c * NSA_NP + k], 0, 0))

    const2 = lambda b, c, pt: (0, 0)
    in_specs = [page_spec(k) for k in range(NSA_NP)] + [
        pl.BlockSpec((None, rows, 2 * HD_B), lambda b, c, pt: (b, 0, 0)),
        pl.BlockSpec((None, T, 6 * HD_B), lambda b, c, pt: (b, 0, 0)),
        pl.BlockSpec((None, lb, 2 * HD_B), lambda b, c, pt: (b, 0, 0)),
        pl.BlockSpec((None, rows, 3), lambda b, c, pt: (b, 0, 0)),
        pl.BlockSpec((GROUP_ROWS, 4 * HD_B), const2),
        pl.BlockSpec((4 * HD_B, 4 * HD_B), const2),
        pl.BlockSpec((1, 4 * HD_B), const2),
        pl.BlockSpec((GROUP_ROWS, 4 * HD_B, 4 * HD_B), lambda b, c, pt: (0, 0, 0)),
        pl.BlockSpec((n_grp, n_blk), const2),
        pl.BlockSpec((NSA_TKS // L_SEL, NSA_TKS), const2),
    ]
    grid_spec = pltpu.PrefetchScalarGridSpec(
        num_scalar_prefetch=1,
        grid=(DB, n_pages // NSA_NP),
        in_specs=in_specs,
        out_specs=pl.BlockSpec((None, T, GROUP_W), lambda b, c, pt: (b, 0, 0)),
        scratch_shapes=[pltpu.VMEM((n_grp, 4 * HD_B), jnp.float32), pltpu.VMEM((2 * HD_B, n_past), bf),
                        pltpu.VMEM((NSA_NP * PAGE_SIZE, 2 * HD_B), jnp.float32)],
    )
    return pl.pallas_call(
        functools.partial(_nsa_sample_kernel, n_pages=n_pages, t_new=T),
        grid_spec=grid_spec,
        out_shape=jax.ShapeDtypeStruct((DB, T, GROUP_W), jnp.float32),
        compiler_params=pltpu.CompilerParams(
            dimension_semantics=("parallel", "arbitrary"), vmem_limit_bytes=VMEM_LIMIT),
        name="nsa_sample",
    )(page_ids, *([pages] * NSA_NP), q_pad, new3, win3, g_t, pe2, bd1, b1x, w2p, cover, eloc)


MLA_NP = 64


def _mla_sample_kernel(pt_ref, *refs, t_new):
    del pt_ref
    f32, bf = jnp.float32, jnp.bfloat16
    page_refs = refs[:MLA_NP]
    qa_ref, qp_ref, new_ref, wuv_ref, o_ref, m_ref, l_ref, acc_ref = refs[MLA_NP:]
    c = pl.program_id(1)
    rows = H_A * t_new

    @pl.when(c == 0)
    def _():
        m_ref[...] = jnp.full(m_ref.shape, NEG, f32)
        l_ref[...] = jnp.zeros(l_ref.shape, f32)
        acc_ref[...] = jnp.zeros(acc_ref.shape, f32)

    qa = qa_ref[...].astype(bf)
    qp = qp_ref[...].astype(bf)

    def fold(kc, kp, mask, k_dims, v_dims):
        s = (lax.dot_general(qa, kc, k_dims, preferred_element_type=f32)
             + lax.dot_general(qp, kp, k_dims, preferred_element_type=f32)) * MLA_SCALE
        if mask is not None:
            s = jnp.where(mask, s, NEG)
        m_old = m_ref[...]
        m_new = jnp.maximum(m_old, jnp.max(s, axis=-1, keepdims=True))
        alpha = jnp.exp(m_old - m_new)
        p = jnp.exp(s - m_new)
        if mask is not None:
            p = jnp.where(mask, p, 0.0)
        l_ref[...] = alpha * l_ref[...] + jnp.sum(p, axis=-1, keepdims=True)
        acc_ref[...] = alpha * acc_ref[...] + lax.dot_general(p.astype(bf), kc, v_dims,
                                                              preferred_element_type=f32)
        m_ref[...] = m_new

    kc = jnp.concatenate([page_refs[k][0:KV_LORA, :] for k in range(MLA_NP)], axis=1).astype(bf)
    kp = jnp.concatenate([page_refs[k][KV_LORA:KV_LORA + ROPE_A, :] for k in range(MLA_NP)], axis=1).astype(bf)
    fold(kc, kp, None, _NN, _NT)

    @pl.when(c == pl.num_programs(1) - 1)
    def _():
        new = jnp.concatenate([new_ref[...], jnp.zeros((LANE - t_new, KV_LORA + ROPE_A), f32)], axis=0)
        tq = lax.broadcasted_iota(jnp.int32, (H_A, t_new, 1), 1).reshape(rows, 1)
        kidx = lax.broadcasted_iota(jnp.int32, (1, LANE), 1)
        fold(new[:, 0:KV_LORA].astype(bf), new[:, KV_LORA:KV_LORA + ROPE_A].astype(bf), kidx <= tq, _NT, _NN)
        o = (acc_ref[...] * (1.0 / jnp.maximum(l_ref[...], 1e-30))).astype(bf)
        for h in range(H_A):
            o_ref[:, h * V_A:(h + 1) * V_A] = jnp.dot(o[h * t_new:(h + 1) * t_new], wuv_ref[h].astype(bf),
                                                      preferred_element_type=f32)


def mla_sample_attend(q_abs, q_pe, rows_new, cache, layer, page_table, w_uv):
    DB, T = q_abs.shape[0], q_abs.shape[1]
    n_pages = page_table.shape[1]
    assert n_pages % MLA_NP == 0 and T % 8 == 0 and T <= LANE
    rows = H_A * T
    dk = KV_LORA + ROPE_A
    qa = q_abs.transpose(0, 2, 1, 3).reshape(DB, rows, KV_LORA)
    qp = q_pe.transpose(0, 2, 1, 3).reshape(DB, rows, ROPE_A)
    pages = cache.transpose(0, 1, 3, 2).reshape(cache.shape[0] * cache.shape[1], dk, PAGE_SIZE)
    page_ids = page_table + layer * cache.shape[1]
    wuv = w_uv.transpose(1, 0, 2)

    def page_spec(k):
        return pl.BlockSpec((None, dk, PAGE_SIZE), lambda b, c, pt: (pt[b, c * MLA_NP + k], 0, 0))

    seq3 = lambda b, c, pt: (b, 0, 0)
    grid_spec = pltpu.PrefetchScalarGridSpec(
        num_scalar_prefetch=1,
        grid=(DB, n_pages // MLA_NP),
        in_specs=[page_spec(k) for k in range(MLA_NP)] + [
            pl.BlockSpec((None, rows, KV_LORA), seq3),
            pl.BlockSpec((None, rows, ROPE_A), seq3),
            pl.BlockSpec((None, T, dk), seq3),
            pl.BlockSpec((H_A, KV_LORA, V_A), lambda b, c, pt: (0, 0, 0)),
        ],
        out_specs=pl.BlockSpec((None, T, GROUP_W), seq3),
        scratch_shapes=[pltpu.VMEM((rows, 1), jnp.float32), pltpu.VMEM((rows, 1), jnp.float32),
                        pltpu.VMEM((rows, KV_LORA), jnp.float32)],
    )
    return pl.pallas_call(
        functools.partial(_mla_sample_kernel, t_new=T),
        grid_spec=grid_spec,
        out_shape=jax.ShapeDtypeStruct((DB, T, GROUP_W), jnp.float32),
        compiler_params=pltpu.CompilerParams(
            dimension_semantics=("parallel", "arbitrary"), vmem_limit_bytes=VMEM_LIMIT),
        name="mla_sample",
    )(page_ids, *([pages] * MLA_NP), qa, qp, rows_new, wuv)


MLA_TQ = 256
MLA_TK = 512


def _mla_prompt_kernel(qa_ref, qp_ref, k_ref, wuv_ref, o_ref, m_ref, l_ref, acc_ref):
    f32, bf = jnp.float32, jnp.bfloat16
    rows = H_A * MLA_TQ
    s0 = pl.program_id(1) * MLA_TQ
    qa = qa_ref[...].reshape(rows, KV_LORA).astype(bf)
    qp = qp_ref[...].reshape(rows, ROPE_A).astype(bf)
    qpos = s0 + lax.broadcasted_iota(jnp.int32, (MLA_TQ, 1), 0)
    m_ref[...] = jnp.full(m_ref.shape, NEG, f32)
    l_ref[...] = jnp.zeros(l_ref.shape, f32)
    acc_ref[...] = jnp.zeros(acc_ref.shape, f32)

    def body(k, carry):
        k0 = pl.multiple_of(k * MLA_TK, MLA_TK)
        kc = k_ref[pl.ds(k0, MLA_TK), 0:KV_LORA].astype(bf)
        kp = k_ref[pl.ds(k0, MLA_TK), KV_LORA:KV_LORA + ROPE_A].astype(bf)
        s = (lax.dot_general(qa, kc, _NT, preferred_element_type=f32)
             + lax.dot_general(qp, kp, _NT, preferred_element_type=f32)) * MLA_SCALE
        kpos = k0 + lax.broadcasted_iota(jnp.int32, (1, MLA_TK), 1)
        mask = (kpos <= qpos)[None]
        s3 = jnp.where(mask, s.reshape(H_A, MLA_TQ, MLA_TK), NEG)
        m_old = m_ref[...].reshape(H_A, MLA_TQ, 1)
        m_new = jnp.maximum(m_old, jnp.max(s3, axis=-1, keepdims=True))
        alpha = jnp.exp(m_old - m_new)
        p = jnp.where(mask, jnp.exp(s3 - m_new), 0.0)
        l_old = l_ref[...].reshape(H_A, MLA_TQ, 1)
        l_ref[...] = (alpha * l_old + jnp.sum(p, axis=-1, keepdims=True)).reshape(rows, 1)
        pv = jnp.dot(p.reshape(rows, MLA_TK).astype(bf), kc, preferred_element_type=f32)
        acc_ref[...] = alpha.reshape(rows, 1) * acc_ref[...] + pv
        m_ref[...] = m_new.reshape(rows, 1)
        return carry

    lax.fori_loop(0, (s0 + MLA_TQ + MLA_TK - 1) // MLA_TK, body, 0)
    o = (acc_ref[...] * (1.0 / jnp.maximum(l_ref[...], 1e-30))).astype(bf)
    for h in range(H_A):
        o_ref[:, h * V_A:(h + 1) * V_A] = jnp.dot(o[h * MLA_TQ:(h + 1) * MLA_TQ], wuv_ref[h].astype(bf),
                                                  preferred_element_type=f32)


def mla_prompt_attend(q_abs, q_pe, rows_kv, w_uv):
    B, S = q_abs.shape[0], q_abs.shape[1]
    assert S % MLA_TK == 0 and MLA_TK % MLA_TQ == 0
    dk = KV_LORA + ROPE_A
    rows = H_A * MLA_TQ
    qa = q_abs.transpose(0, 2, 1, 3)
    qp = q_pe.transpose(0, 2, 1, 3)
    wuv = w_uv.transpose(1, 0, 2)
    return pl.pallas_call(
        _mla_prompt_kernel,
        grid=(B, S // MLA_TQ),
        in_specs=[
            pl.BlockSpec((None, H_A, MLA_TQ, KV_LORA), lambda b, i: (b, 0, i, 0)),
            pl.BlockSpec((None, H_A, MLA_TQ, ROPE_A), lambda b, i: (b, 0, i, 0)),
            pl.BlockSpec((None, S, dk), lambda b, i: (b, 0, 0)),
            pl.BlockSpec((H_A, KV_LORA, V_A), lambda b, i: (0, 0, 0)),
        ],
        out_specs=pl.BlockSpec((None, MLA_TQ, GROUP_W), lambda b, i: (b, i, 0)),
        out_shape=jax.ShapeDtypeStruct((B, S, GROUP_W), jnp.float32),
        scratch_shapes=[pltpu.VMEM((rows, 1), jnp.float32), pltpu.VMEM((rows, 1), jnp.float32),
                        pltpu.VMEM((rows, KV_LORA), jnp.float32)],
        compiler_params=pltpu.CompilerParams(
            dimension_semantics=("parallel", "arbitrary"), vmem_limit_bytes=VMEM_LIMIT),
        name="mla_prompt",
    )(qa, qp, rows_kv, wuv)


def _compress_kernel(x_ref, pe_ref, bd1_ref, b1_ref, w2p_ref, o_ref):
    f32, bf = jnp.float32, jnp.bfloat16
    n_grp = o_ref.shape[0]
    ab = jnp.zeros((n_grp, 4 * HD_B), f32)
    for p in range(GROUP_ROWS):
        xp = x_ref[pl.ds(p, n_grp, stride=GROUP_ROWS), :]
        x2 = jnp.concatenate([xp, xp], axis=1) + pe_ref[p:p + 1, :]
        h = jax.nn.gelu(jnp.dot(x2.astype(bf), bd1_ref[...], preferred_element_type=f32) + b1_ref[...])
        ab = ab + jnp.dot(h.astype(bf), w2p_ref[p], preferred_element_type=f32)
    o_ref[...] = ab[:, 0:2 * HD_B] + pltpu.roll(ab[:, 2 * HD_B:4 * HD_B], n_grp - 1, 0)


def _compress_weights(pe, w1, b1, w2):
    bf = jnp.bfloat16
    pe_kv = jnp.concatenate([pe[0], pe[1]], axis=-1)
    pe2 = jnp.concatenate([pe_kv[:GROUP_ROWS], pe_kv[GROUP_ROWS:]], axis=-1)
    bd1 = _blockdiag([w1[0], w1[1], w1[0], w1[1]]).astype(bf)
    b1x = jnp.concatenate([b1[0], b1[1], b1[0], b1[1]]).reshape(1, 4 * HD_B)
    w2r = w2.reshape(2, L_CMP, HD_B, HD_B)
    w2p = jnp.stack([_blockdiag([w2r[0, p], w2r[1, p], w2r[0, GROUP_ROWS + p], w2r[1, GROUP_ROWS + p]])
                     for p in range(GROUP_ROWS)]).astype(bf)
    return pe2, bd1, b1x, w2p


def compress_prompt(kv3, pe, w1, b1, w2):
    B, S = kv3.shape[0], kv3.shape[1]
    assert S % (8 * GROUP_ROWS) == 0
    n_grp = S // GROUP_ROWS
    pe2, bd1, b1x, w2p = _compress_weights(pe, w1, b1, w2)
    const2 = lambda b: (0, 0)
    return pl.pallas_call(
        _compress_kernel,
        grid=(B,),
        in_specs=[
            pl.BlockSpec((None, S, 2 * HD_B), lambda b: (b, 0, 0)),
            pl.BlockSpec((GROUP_ROWS, 4 * HD_B), const2),
            pl.BlockSpec((4 * HD_B, 4 * HD_B), const2),
            pl.BlockSpec((1, 4 * HD_B), const2),
            pl.BlockSpec((GROUP_ROWS, 4 * HD_B, 4 * HD_B), lambda b: (0, 0, 0)),
        ],
        out_specs=pl.BlockSpec((None, n_grp, 2 * HD_B), lambda b: (b, 0, 0)),
        out_shape=jax.ShapeDtypeStruct((B, n_grp, 2 * HD_B), jnp.float32),
        compiler_params=pltpu.CompilerParams(dimension_semantics=("parallel",), vmem_limit_bytes=VMEM_LIMIT),
        name="nsa_compress",
    )(kv3, pe2, bd1, b1x, w2p)


POOL_HALO = 16
assert POOL_HALO >= max(POOL_WINDOWS) and POOL_HALO % 8 == 0


def _pool_kernel(x_ref, prev_ref, w_ref, sc_ref, o_ref, *, tq, pos0, zero_first_prev):
    f32, bf = jnp.float32, jnp.bfloat16
    i = pl.program_id(1)
    cur = x_ref[...]
    prev = prev_ref[...]
    if zero_first_prev:
        prev = jnp.where(i == 0, 0.0, prev)
    ext = jnp.concatenate([prev, cur], axis=0)
    pos = pos0 + i * tq + lax.broadcasted_iota(jnp.int32, (tq, 1), 0)
    sums = {1: ext}
    w = 1
    while w < max(POOL_WINDOWS):
        sums[2 * w] = sums[w] + pltpu.roll(sums[w], w, 0)
        w *= 2
    for g, win in enumerate(POOL_WINDOWS):
        c0, c1 = g * POOL_GW, (g + 1) * POOL_GW
        cnt = jnp.minimum(pos + 1, win).astype(f32)
        pooled = sums[win][POOL_HALO:, c0:c1] / cnt - cur[:, c0:c1]
        y = jnp.dot(pooled.astype(bf), w_ref[g].astype(bf), preferred_element_type=f32)
        o_ref[:, c0:c1] = y * sc_ref[:, c0:c1]


def pool_attend(x, prev, w, scale, pos0):
    B, L = x.shape[0], x.shape[1]
    tq = min(L, 256)
    assert L % tq == 0 and tq % 8 == 0 and tq % POOL_HALO in (0, 8)
    zero_first = prev is None
    if zero_first:
        assert tq % POOL_HALO == 0
        hb = tq // POOL_HALO
        prev_arr = x
        prev_spec = pl.BlockSpec((None, POOL_HALO, GROUP_W), lambda b, i: (b, jnp.maximum(i * hb - 1, 0), 0))
    else:
        assert L == tq
        prev_arr = prev
        prev_spec = pl.BlockSpec((None, POOL_HALO, GROUP_W), lambda b, i: (b, 0, 0))
    return pl.pallas_call(
        functools.partial(_pool_kernel, tq=tq, pos0=pos0, zero_first_prev=zero_first),
        grid=(B, L // tq),
        in_specs=[
            pl.BlockSpec((None, tq, GROUP_W), lambda b, i: (b, i, 0)),
            prev_spec,
            pl.BlockSpec((len(POOL_WINDOWS), POOL_GW, POOL_GW), lambda b, i: (0, 0, 0)),
            pl.BlockSpec((1, GROUP_W), lambda b, i: (0, 0)),
        ],
        out_specs=pl.BlockSpec((None, tq, GROUP_W), lambda b, i: (b, i, 0)),
        out_shape=jax.ShapeDtypeStruct((B, L, GROUP_W), jnp.float32),
        compiler_params=pltpu.CompilerParams(
            dimension_semantics=("parallel", "arbitrary"), vmem_limit_bytes=VMEM_LIMIT),
        name="pool",
    )(x, prev_arr, w, scale.reshape(1, GROUP_W))


def _gmlp_kernel(z_ref, vn_ref, w_ref, b_ref, o_ref, v_ref):
    f32, bf = jnp.float32, jnp.bfloat16
    a = jax.nn.gelu(z_ref[...])
    u, v = a[:, :GROUP_W], a[:, GROUP_W:]
    v = v * lax.rsqrt(jnp.mean(v * v, axis=-1, keepdims=True) + EPS) * vn_ref[...]
    v_ref[...] = v
    vb = v.astype(bf)
    for g in range(G_D):
        c0, c1 = g * GD_W, (g + 1) * GD_W
        s = jnp.dot(w_ref[g].astype(bf), vb[:, c0:c1], preferred_element_type=f32) + b_ref[g]
        o_ref[:, c0:c1] = u[:, c0:c1] * s


def gmlp_attend(zd, v_norm, ws, bs):
    B, L = zd.shape[0], zd.shape[1]
    n = min(L, CHUNK)
    wm = ws[:, :n, :n] * jnp.tril(jnp.ones((n, n), ws.dtype))
    bias = bs[:, :n]
    if n < CHUNK:
        assert CHUNK % n == 0 and (B * L) % CHUNK == 0 and L == n
        rep = CHUNK // n
        wm = jnp.stack([jnp.kron(jnp.eye(rep, dtype=ws.dtype), wm[g]) for g in range(G_D)])
        bias = jnp.tile(bias, (1, rep))
    else:
        assert L % CHUNK == 0
    rows = B * L
    od, v = pl.pallas_call(
        _gmlp_kernel,
        grid=(rows // CHUNK,),
        in_specs=[
            pl.BlockSpec((CHUNK, 2 * GROUP_W), lambda i: (i, 0)),
            pl.BlockSpec((1, GROUP_W), lambda i: (0, 0)),
            pl.BlockSpec((G_D, CHUNK, CHUNK), lambda i: (0, 0, 0)),
            pl.BlockSpec((G_D, CHUNK, 1), lambda i: (0, 0, 0)),
        ],
        out_specs=[pl.BlockSpec((CHUNK, GROUP_W), lambda i: (i, 0)),
                   pl.BlockSpec((CHUNK, GROUP_W), lambda i: (i, 0))],
        out_shape=[jax.ShapeDtypeStruct((rows, GROUP_W), jnp.float32),
                   jax.ShapeDtypeStruct((rows, GROUP_W), jnp.float32)],
        compiler_params=pltpu.CompilerParams(dimension_semantics=("parallel",), vmem_limit_bytes=VMEM_LIMIT),
        name="gmlp",
    )(zd.reshape(rows, 2 * GROUP_W), v_norm.reshape(1, GROUP_W), wm, bias[:, :, None])
    return od.reshape(B, L, GROUP_W), v.reshape(B, L, GROUP_W)


def _mem_attn_kernel(q_ref, kv_ref, o_ref):
    f32, bf = jnp.float32, jnp.bfloat16
    q = q_ref[...].astype(bf)
    kv = kv_ref[...].astype(bf)
    for h in range(H_M):
        c0, c1 = h * HD_M, (h + 1) * HD_M
        s = lax.dot_general(q[:, c0:c1], kv[:, c0:c1], _NT, preferred_element_type=f32) * MEM_SCALE
        e = jnp.exp(s - jnp.max(s, axis=-1, keepdims=True))
        p = e * (1.0 / jnp.sum(e, axis=-1, keepdims=True))
        o_ref[:, c0:c1] = jnp.dot(p.astype(bf), kv[:, H_M * HD_M + c0:H_M * HD_M + c1],
                                  preferred_element_type=f32)


def mem_attn_core(q, mkv):
    B, T, D = q.shape
    M = mkv.shape[1]
    tq = min(T, 512)
    assert T % tq == 0
    return pl.pallas_call(
        _mem_attn_kernel,
        grid=(B, T // tq),
        in_specs=[pl.BlockSpec((None, tq, D), lambda b, i: (b, i, 0)),
                  pl.BlockSpec((None, M, 2 * D), lambda b, i: (b, 0, 0))],
        out_specs=pl.BlockSpec((None, tq, D), lambda b, i: (b, i, 0)),
        out_shape=jax.ShapeDtypeStruct((B, T, D), jnp.float32),
        compiler_params=pltpu.CompilerParams(
            dimension_semantics=("parallel", "arbitrary"), vmem_limit_bytes=VMEM_LIMIT),
        name="mem_attn",
    )(q, mkv.reshape(B, M, 2 * D))


CONV_HALO = 8
assert CONV_HALO >= CONV_W - 1


def _conv_gate_kernel(ag_ref, au_ref, pg_ref, pu_ref, wg_ref, wu_ref, bg_ref, bu_ref, o_ref, *, zero_first_prev):
    i = pl.program_id(1)

    def conv(a_ref, p_ref, w_ref, b_ref):
        cur = a_ref[...]
        prev = p_ref[...]
        if zero_first_prev:
            prev = jnp.where(i == 0, 0.0, prev)
        ext = jnp.concatenate([prev, cur], axis=0)
        c = b_ref[...] + cur * w_ref[CONV_W - 1:CONV_W, :]
        for k in range(1, CONV_W):
            c = c + pltpu.roll(ext, k, 0)[CONV_HALO:] * w_ref[CONV_W - 1 - k:CONV_W - k, :]
        return c

    g = conv(ag_ref, pg_ref, wg_ref, bg_ref)
    u = conv(au_ref, pu_ref, wu_ref, bu_ref)
    o_ref[...] = (jax.nn.silu(g) * u).astype(o_ref.dtype)


def conv_gate(a, prev, conv_w, conv_b):
    B, L = a.shape[0], a.shape[1]
    tq = min(L, 512)
    tf = 512 if L >= 512 else D_FF
    assert L % tq == 0 and D_FF % tf == 0 and tq % CONV_HALO == 0
    nf = D_FF // tf
    zero_first = prev is None
    if zero_first:
        hb = tq // CONV_HALO
        prev_arr = a
        pidx = lambda i: jnp.maximum(i * hb - 1, 0)
    else:
        assert L == tq
        prev_arr = prev
        pidx = lambda i: 0
    cw = conv_w.reshape(CONV_W, 2 * D_FF)
    cb = conv_b.reshape(1, 2 * D_FF)
    return pl.pallas_call(
        functools.partial(_conv_gate_kernel, zero_first_prev=zero_first),
        grid=(B, L // tq, nf),
        in_specs=[
            pl.BlockSpec((None, tq, tf), lambda b, i, f: (b, i, f)),
            pl.BlockSpec((None, tq, tf), lambda b, i, f: (b, i, nf + f)),
            pl.BlockSpec((None, CONV_HALO, tf), lambda b, i, f: (b, pidx(i), f)),
            pl.BlockSpec((None, CONV_HALO, tf), lambda b, i, f: (b, pidx(i), nf + f)),
            pl.BlockSpec((CONV_W, tf), lambda b, i, f: (0, f)),
            pl.BlockSpec((CONV_W, tf), lambda b, i, f: (0, nf + f)),
            pl.BlockSpec((1, tf), lambda b, i, f: (0, f)),
            pl.BlockSpec((1, tf), lambda b, i, f: (0, nf + f)),
        ],
        out_specs=pl.BlockSpec((None, tq, tf), lambda b, i, f: (b, i, f)),
        out_shape=jax.ShapeDtypeStruct((B, L, D_FF), jnp.bfloat16),
        compiler_params=pltpu.CompilerParams(
            dimension_semantics=("parallel", "arbitrary", "arbitrary"), vmem_limit_bytes=VMEM_LIMIT),
        name="conv_gate",
    )(a, a, prev_arr, prev_arr, cw, cw, cb, cb)


def rmsnorm(x, g):
    xf = x.astype(jnp.float32)
    y = xf * lax.rsqrt(jnp.mean(xf * xf, axis=-1, keepdims=True) + EPS)
    return (y * g.astype(jnp.float32)).astype(x.dtype)


def rope(x, pos, rot):
    half = rot // 2
    inv = ROPE_THETA ** (-2.0 * jnp.arange(half, dtype=jnp.float32) / rot)
    ang = pos.astype(jnp.float32)[:, None] * inv[None, :]
    shape = (pos.shape[0],) + (1,) * (x.ndim - 3) + (half,)
    cos, sin = jnp.cos(ang).reshape(shape), jnp.sin(ang).reshape(shape)
    xf = x.astype(jnp.float32)
    x1, x2 = xf[..., :half], xf[..., half:rot]
    return jnp.concatenate([x1 * cos - x2 * sin, x2 * cos + x1 * sin, xf[..., rot:]], axis=-1).astype(x.dtype)


def masked_softmax(s, mask):
    s = jnp.where(mask, s.astype(jnp.float32), -jnp.inf)
    m = jnp.max(s, axis=-1, keepdims=True)
    e = jnp.exp(s - jnp.where(jnp.isfinite(m), m, 0.0))
    return e / jnp.maximum(jnp.sum(e, axis=-1, keepdims=True), 1e-30)


def project(x, pos, norm_g, w_in_p, q_norm, kv_norm, w_uq, w_uk, gate_bias):
    B, T = x.shape[0], x.shape[1]
    z = dense(x.reshape(B * T, D_MODEL), w_in_p, norm_g=norm_g, tn_cap=640).reshape(B, T, P_IN_PAD)
    idx, acc = [], 0
    for n in IN_SIZES:
        acc += n
        idx.append(acc)
    c_q, c_kv, k_pe, q_b, kv_b, g_b, x_c, x_d, _ = jnp.split(z, idx, axis=-1)
    qa = jnp.einsum('btr,rhd->bthd', rmsnorm(c_q, q_norm), w_uq)
    q_abs = jnp.einsum('bthd,rhd->bthr', qa[..., :NOPE_A], w_uk)
    q_pe = rope(qa[..., NOPE_A:], pos, ROPE_A)
    ckv = rmsnorm(c_kv, kv_norm)
    kpe = rope(k_pe, pos, ROPE_A)
    qn = rope(q_b.reshape(B, T, H_B, HD_B), pos, ROT_B)
    kv = kv_b.reshape(B, T, 3, 2, HD_B)
    kv = jnp.stack([rope(kv[:, :, :, 0], pos, ROT_B), kv[:, :, :, 1]], axis=3).reshape(B, T, 6, HD_B)
    gates = jax.nn.sigmoid(g_b + gate_bias).reshape(B, T, H_B, 3)
    return q_abs, q_pe, ckv, kpe, qn, kv, gates, x_c, x_d


def mla_attend(q_abs, q_pe, ckv, kpe, mask):
    s = (jnp.einsum('...thr,...kr->...htk', q_abs, ckv)
         + jnp.einsum('...thp,...kp->...htk', q_pe, kpe)) * MLA_SCALE
    p = masked_softmax(s, mask)
    return jnp.einsum('...htk,...kr->...thr', p.astype(ckv.dtype), ckv)


def mla_prompt(q_abs, q_pe, ckv, kpe):
    B, S = q_abs.shape[0], q_abs.shape[1]
    nb = S // Q_BLOCK
    qb = q_abs.reshape(B, nb, Q_BLOCK, H_A, KV_LORA).swapaxes(0, 1)
    pb = q_pe.reshape(B, nb, Q_BLOCK, H_A, ROPE_A).swapaxes(0, 1)
    k_pos = jnp.arange(S)

    def block(args):
        qa, qp, i = args
        q_pos = i * Q_BLOCK + jnp.arange(Q_BLOCK)
        return mla_attend(qa, qp, ckv, kpe, k_pos[None, :] <= q_pos[:, None])

    o = lax.map(block, (qb, pb, jnp.arange(nb)))
    return o.swapaxes(0, 1).reshape(B, S, H_A, KV_LORA)


def mla_sample(q_abs, q_pe, ckv_new, kpe_new, cache, layer, page_table):
    T = q_abs.shape[1]
    q_pos = PAST_LEN + jnp.arange(T)
    k_pos = jnp.arange(PAST_LEN + T)
    mask = k_pos[None, :] <= q_pos[:, None]

    def one(args):
        qa, qp, cn, kn, pt = args
        rows = cache[layer, pt].reshape(-1, KV_LORA + ROPE_A)
        ckv = jnp.concatenate([rows[:, :KV_LORA].astype(cn.dtype), cn], axis=0)
        kpe = jnp.concatenate([rows[:, KV_LORA:].astype(kn.dtype), kn], axis=0)
        return mla_attend(qa, qp, ckv, kpe, mask)

    return lax.map(one, (q_abs, q_pe, ckv_new, kpe_new, page_table))


def mla_out(o_lat, w_uv):
    B, T = o_lat.shape[0], o_lat.shape[1]
    return jnp.einsum('bthr,rhd->bthd', o_lat, w_uv).reshape(B, T, GROUP_W)


def nsa_compress(k, pe, w1, b1, w2):
    n_cmp = (k.shape[0] - L_CMP) // D_CMP + 1
    idx = (jnp.arange(n_cmp) * D_CMP)[:, None] + jnp.arange(L_CMP)[None, :]
    hid = jax.nn.gelu((k[idx] + pe) @ w1 + b1)
    return hid.reshape(n_cmp, L_CMP * HD_B) @ w2


def compress_kv(k, v, pe, w1, b1, w2):
    return (nsa_compress(k, pe[0], w1[0], b1[0], w2[0]), nsa_compress(v, pe[1], w1[1], b1[1], w2[1]))


def nsa_cmp_sel(q, q_pos, kc, vc, k_sel, v_sel):
    T = q.shape[0]
    n_cmp, n = kc.shape[0], k_sel.shape[0]
    n_blk = -(-n // L_SEL)
    c_end = jnp.arange(n_cmp) * D_CMP + (L_CMP - 1)
    p_c = masked_softmax(jnp.einsum('thd,nd->htn', q, kc) * NSA_SCALE, c_end[None, :] <= q_pos[:, None])
    o_cmp = jnp.einsum('htn,nd->thd', p_c.astype(vc.dtype), vc)
    ci = jnp.arange(n_cmp)[:, None] * D_CMP
    bj = jnp.arange(n_blk)[None, :] * L_SEL
    cover = ((ci < bj + L_SEL) & (ci + L_CMP > bj)).astype(jnp.float32)
    imp = jnp.sum(p_c, axis=0) @ cover
    cur = (q_pos // L_SEL)[:, None]
    j = jnp.arange(n_blk)[None, :]
    causal = j <= cur
    forced = (j == 0) | (j == cur) | (j == cur - 1)
    score = jnp.where(forced, FORCE, jnp.where(causal, imp, -FORCE))
    k_top = min(N_SEL, n_blk)
    _, sel = lax.top_k(score, k_top)
    sel_ok = jnp.take_along_axis(causal, sel, axis=1)
    pos = (sel[:, :, None] * L_SEL + jnp.arange(L_SEL)[None, None, :]).reshape(T, k_top * L_SEL)
    ok = jnp.repeat(sel_ok, L_SEL, axis=1) & (pos <= q_pos[:, None])
    pos_c = jnp.minimum(pos, n - 1)
    kg, vg = k_sel[pos_c], v_sel[pos_c]
    p_s = masked_softmax(jnp.einsum('thd,tkd->htk', q, kg) * NSA_SCALE, ok[None])
    o_sel = jnp.einsum('htk,tkd->thd', p_s.astype(vg.dtype), vg)
    return o_cmp, o_sel


def window_attend(q, q_pos, k, v, k_pos):
    ok = ((k_pos[None, :] <= q_pos[:, None]) & (k_pos[None, :] > q_pos[:, None] - WINDOW)
          & (k_pos[None, :] >= 0))
    p = masked_softmax(jnp.einsum('bthd,bkd->bhtk', q, k) * NSA_SCALE, ok)
    return jnp.einsum('bhtk,bkd->bthd', p.astype(v.dtype), v)


def nsa_prompt(q, kv, pe, w1, b1, w2):
    B, S = q.shape[0], q.shape[1]
    kc, vc = jax.vmap(compress_kv, in_axes=(0, 0, None, None, None, None))(kv[:, :, 0], kv[:, :, 1], pe, w1, b1, w2)
    k_sel, v_sel = kv[:, :, 2], kv[:, :, 3]
    pad = ((0, 0), (WINDOW, 0), (0, 0))
    k_win, v_win = jnp.pad(kv[:, :, 4], pad), jnp.pad(kv[:, :, 5], pad)
    nb = S // Q_BLOCK
    qb = q.reshape(B, nb, Q_BLOCK, H_B, HD_B).swapaxes(0, 1)
    sparse = jax.vmap(nsa_cmp_sel, in_axes=(0, None, 0, 0, 0, 0))

    def block(args):
        qq, i = args
        start = i * Q_BLOCK
        q_pos = start + jnp.arange(Q_BLOCK)
        o_c, o_s = sparse(qq, q_pos, kc, vc, k_sel, v_sel)
        k_pos = start - WINDOW + jnp.arange(WINDOW + Q_BLOCK)
        kb = lax.dynamic_slice_in_dim(k_win, start, WINDOW + Q_BLOCK, axis=1)
        vb = lax.dynamic_slice_in_dim(v_win, start, WINDOW + Q_BLOCK, axis=1)
        return o_c, o_s, window_attend(qq, q_pos, kb, vb, k_pos)

    o_c, o_s, o_w = lax.map(block, (qb, jnp.arange(nb)))
    unblock = lambda o: o.swapaxes(0, 1).reshape(B, S, H_B, HD_B)
    return unblock(o_c), unblock(o_s), unblock(o_w)


def nsa_sample(q, kv, cache, layer, page_table, win_buf, pe, w1, b1, w2):
    T = q.shape[1]
    q_pos = PAST_LEN + jnp.arange(T)

    def one(args):
        qq, kn, pt = args
        rows = cache[layer, pt].reshape(-1, 4, HD_B).astype(kn.dtype)
        full = jnp.concatenate([rows, kn[:, :4]], axis=0)
        kc, vc = compress_kv(full[:, 0], full[:, 1], pe, w1, b1, w2)
        return nsa_cmp_sel(qq, q_pos, kc, vc, full[:, 2], full[:, 3])

    o_c, o_s = lax.map(one, (q, kv, page_table))
    lb = win_buf.shape[1]
    wkv = jnp.concatenate([win_buf.astype(kv.dtype), kv[:, :, 4:6]], axis=1)
    k_pos = PAST_LEN - lb + jnp.arange(lb + T)
    o_w = window_attend(q, q_pos, wkv[:, :, 0], wkv[:, :, 1], k_pos)
    return o_c, o_s, o_w, wkv[:, -lb:]


def nsa_merge(gates, o_c, o_s, o_w):
    B, T = o_c.shape[0], o_c.shape[1]
    o = gates[..., 0:1] * o_c + gates[..., 1:2] * o_s + gates[..., 2:3] * o_w
    return o.reshape(B, T, GROUP_W)


def pool_mix(xs, w, scale):
    B, L = xs.shape[0], xs.shape[1]
    xf = xs.astype(jnp.float32)
    cs0 = jnp.pad(jnp.cumsum(xf, axis=1), ((0, 0), (1, 0), (0, 0)))
    r = jnp.arange(L)
    parts = []
    for g, win in enumerate(POOL_WINDOWS):
        c0, c1 = g * POOL_GW, (g + 1) * POOL_GW
        start = jnp.maximum(r + 1 - win, 0)
        cnt = (r + 1 - start).astype(jnp.float32)[None, :, None]
        mean = (cs0[:, 1:, c0:c1] - cs0[:, start, c0:c1]) / cnt
        parts.append(mean - xf[:, :, c0:c1])
    pooled = jnp.stack(parts, axis=2).astype(xs.dtype)
    y = jnp.einsum('blgc,gcd->blgd', pooled, w).reshape(B, L, GROUP_W)
    return y * scale


def gmlp_mix(zd, v_norm, ws, bs):
    B, L = zd.shape[0], zd.shape[1]
    u, v = jnp.split(jax.nn.gelu(zd), 2, axis=-1)
    v = rmsnorm(v, v_norm)
    n = min(L, CHUNK)
    wm = ws[:, :n, :n] * jnp.tril(jnp.ones((n, n), ws.dtype))
    vc = v.reshape(B, L // n, n, G_D, GD_W)
    s = jnp.einsum('gts,bcsgd->bctgd', wm, vc) + bs[:, :n].T[None, None, :, :, None]
    return u * s.reshape(B, L, GROUP_W), v


def mem_attend(x, norm_g, mkv, wq, wo, layer):
    B, T = x.shape[0], x.shape[1]
    x2 = x.reshape(B * T, D_MODEL)
    q = dense(x2, wq, layer, norm_g=norm_g).reshape(B, T, H_M * HD_M)
    o = mem_attn_core(q, mkv).reshape(B * T, H_M * HD_M)
    return dense(o, wo, layer, residual=x2).reshape(B, T, D_MODEL)


def conv_ffn(x, norm_g, buf, w_up, conv_w, conv_b, w_down, layer):
    B, L = x.shape[0], x.shape[1]
    x2 = x.reshape(B * L, D_MODEL)
    assert L >= CONV_W - 1
    a = dense(x2, w_up, layer, norm_g=norm_g, tn_cap=1024).reshape(B, L, 2 * D_FF)
    prev = None if buf is None else jnp.pad(buf.astype(a.dtype), ((0, 0), (CONV_HALO - (CONV_W - 1), 0), (0, 0)))
    act = conv_gate(a, prev, conv_w, conv_b).reshape(B * L, D_FF)
    y = dense(act, w_down, layer, residual=x2, tm=512)
    return y.reshape(B, L, D_MODEL), a[:, L - (CONV_W - 1):]


def kernel(x_prompt, x_sample, cache_mla, cache_nsa, state_nsa_win, state_pool, state_conv, cache_mem_kv,
           page_table, mem_prompt, mix_norm, w_in, mla_q_norm, mla_kv_norm, mla_w_uq, mla_w_ukv,
           nsa_gate_bias, nsa_cmp_pe, nsa_cmp_w1, nsa_cmp_b1, nsa_cmp_w2, pool_w, pool_scale,
           gmlp_v_norm, gmlp_ws, gmlp_bs, mix_out_norm, w_out, mem_norm, mem_kv_norm, mem_wq, mem_wkv,
           mem_wo, ffn_norm, ffn_w_up, ffn_conv_w, ffn_conv_b, ffn_w_down, final_norm):
    B, S = x_prompt.shape[0], x_prompt.shape[1]
    DB, T = x_sample.shape[0], x_sample.shape[1]
    pos_p = jnp.arange(S, dtype=jnp.int32)
    pos_s = PAST_LEN + jnp.arange(T, dtype=jnp.int32)
    lb_p = min(WINDOW, S)
    xp, xs = x_prompt, x_sample
    mla_p, mla_s, nsa_p, nsa_s, win_p, win_s = [], [], [], [], [], []
    pool_p, pool_s, gv_s, conv_p, conv_s, memkv_p = [], [], [], [], [], []
    for l in range(DEPTH):
        w_uk = mla_w_ukv[l][..., :NOPE_A]
        w_uv = mla_w_ukv[l][..., NOPE_A:]
        cmp = (nsa_cmp_pe[l], nsa_cmp_w1[l], nsa_cmp_b1[l], nsa_cmp_w2[l])
        w_in_p = jnp.pad(w_in[l], ((0, 0), (0, P_IN_PAD - P_IN)))

        def mix_merge(x, oa, ob, oc, od):
            Bx, Tx = x.shape[0], x.shape[1]
            o = jnp.concatenate([oa, ob, oc, od], axis=-1).reshape(Bx * Tx, D_MODEL)
            return dense(o, w_out, l, norm_g=mix_out_norm[l], n_groups=4,
                         residual=x.reshape(Bx * Tx, D_MODEL)).reshape(Bx, Tx, D_MODEL)

        qa, qpe, ckv, kpe, qb, kv, gt, xc, xd = project(xp, pos_p, mix_norm[l], w_in_p, mla_q_norm[l],
                                                        mla_kv_norm[l], mla_w_uq[l], w_uk, nsa_gate_bias[l])
        rows_p = jnp.concatenate([ckv, kpe], axis=-1)
        oa = mla_prompt_attend(qa, qpe, rows_p, w_uv)
        kcvc = compress_prompt(kv.reshape(B, S, 6 * HD_B), *cmp)
        ob = nsa_prompt_attend(qb, kv, kcvc, (S - L_CMP) // D_CMP + 1, gt)
        oc = pool_attend(xc, None, pool_w[l], pool_scale[l], 0)
        od, _ = gmlp_attend(xd, gmlp_v_norm[l], gmlp_ws[l], gmlp_bs[l])
        xp = mix_merge(xp, oa, ob, oc, od)
        mla_p.append(rows_p)
        nsa_p.append(kv[:, :, :4])
        win_p.append(kv[:, S - lb_p:, 4:6])
        pool_p.append(xc[:, S - POOL_BUF:])

        qa, qpe, ckv, kpe, qb, kv, gt, xc, xd = project(xs, pos_s, mix_norm[l], w_in_p, mla_q_norm[l],
                                                        mla_kv_norm[l], mla_w_uq[l], w_uk, nsa_gate_bias[l])
        rows_s = jnp.concatenate([ckv, kpe], axis=-1)
        oa = mla_sample_attend(qa, qpe, rows_s, cache_mla, l, page_table, w_uv)
        ob = nsa_sample_attend(qb, kv, gt, cache_nsa, l, page_table, state_nsa_win[l], *cmp)
        new_win = jnp.concatenate([state_nsa_win[l], kv[:, :, 4:6]], axis=1)[:, T:]
        xcat = jnp.concatenate([state_pool[l].astype(xc.dtype), xc], axis=1)
        prev_s = jnp.pad(state_pool[l].astype(xc.dtype), ((0, 0), (POOL_HALO - POOL_BUF, 0), (0, 0)))
        oc = pool_attend(xc, prev_s, pool_w[l], pool_scale[l], POOL_BUF)
        od, v_rows = gmlp_attend(xd, gmlp_v_norm[l], gmlp_ws[l], gmlp_bs[l])
        xs = mix_merge(xs, oa, ob, oc, od)
        mla_s.append(rows_s)
        nsa_s.append(kv[:, :, :4])
        win_s.append(new_win)
        pool_s.append(xcat[:, xcat.shape[1] - POOL_BUF:])
        gv_s.append(v_rows)

        mkv = dense(mem_prompt.reshape(B * MEM_LEN, D_MODEL), mem_wkv, l,
                    norm_g=mem_kv_norm[l]).reshape(B, MEM_LEN, 2, H_M * HD_M)
        xp = mem_attend(xp, mem_norm[l], mkv, mem_wq, mem_wo, l)
        xs = mem_attend(xs, mem_norm[l], cache_mem_kv[l].astype(xs.dtype), mem_wq, mem_wo, l)
        memkv_p.append(mkv)

        xp, cbp = conv_ffn(xp, ffn_norm[l], None, ffn_w_up, ffn_conv_w[l], ffn_conv_b[l], ffn_w_down, l)
        xs, cbs = conv_ffn(xs, ffn_norm[l], state_conv[l], ffn_w_up, ffn_conv_w[l], ffn_conv_b[l], ffn_w_down, l)
        conv_p.append(cbp)
        conv_s.append(cbs)

    y_prompt = rmsnorm(xp, final_norm)
    y_sample = rmsnorm(xs, final_norm)
    return (y_prompt, y_sample, jnp.stack(mla_p), jnp.stack(mla_s), jnp.stack(nsa_p), jnp.stack(nsa_s),
            jnp.stack(win_p), jnp.stack(win_s), jnp.stack(pool_p), jnp.stack(pool_s), jnp.stack(gv_s),
            jnp.stack(conv_p), jnp.stack(conv_s), jnp.stack(memkv_p))
```
